```python
import jax, jax.numpy as jnp
from jax import lax
import numpy as np

D_MODEL = 2048
BATCH = 2
SEQ = 16384
DEPTH = 4

HEAD_DIM = 128
N_FOX = D_MODEL // (2 * HEAD_DIM)
N_RET = D_MODEL // (2 * HEAD_DIM)
MLSTM_V = 256
MLSTM_QK = MLSTM_V // 2
N_MLSTM = D_MODEL // MLSTM_V
N_XATTN = 4
XATTN_DIM = D_MODEL // N_XATTN
N_MEM = 256
D_FF = 4 * D_MODEL
BLOCK = 128
CONV_W = 4
ROPE_BASE = 10000.0
EPS = 1e-6
FOX_W = N_FOX * HEAD_DIM
RET_W = N_RET * HEAD_DIM
MIX_W = FOX_W + RET_W
EV_IN = 3 * FOX_W + N_FOX + 4 * RET_W
QK_W = N_MLSTM * MLSTM_QK
V_W = N_MLSTM * MLSTM_V
OD_IN = 2 * QK_W + 2 * V_W + 2 * N_MLSTM
N_EVEN = (DEPTH + 1) // 2
N_ODD = DEPTH // 2

kernel_name = "fox_retention_mlstm_hybrid_trunk"


def rms_norm(x, g):
    xf = x.astype(jnp.float32)
    y = xf * lax.rsqrt(jnp.mean(xf * xf, axis=-1, keepdims=True) + EPS)
    return (y * g.astype(jnp.float32)).astype(x.dtype)


def split_cols(t, widths):
    idx = np.cumsum(widths)[:-1].tolist()
    return jnp.split(t, idx, axis=-1)


def split_heads(t, n_heads):
    b, s, _ = t.shape
    return t.reshape(b, s, n_heads, -1).transpose(0, 2, 1, 3)


def merge_heads(t):
    b, h, s, d = t.shape
    return t.transpose(0, 2, 1, 3).reshape(b, s, h * d)


def rope(t):
    s, d = t.shape[2], t.shape[3]
    inv = 1.0 / (ROPE_BASE ** (jnp.arange(0, d, 2, dtype=jnp.float32) / d))
    ang = jnp.arange(s, dtype=jnp.float32)[:, None] * inv[None, :]
    cos, sin = jnp.cos(ang).astype(t.dtype), jnp.sin(ang).astype(t.dtype)
    t1, t2 = t[..., : d // 2], t[..., d // 2:]
    return jnp.concatenate([t1 * cos - t2 * sin, t1 * sin + t2 * cos], axis=-1)


def to_chunks(t, nc):
    b, h = t.shape[0], t.shape[1]
    return jnp.moveaxis(t.reshape(b, h, nc, BLOCK, *t.shape[3:]), 2, 0)


def from_chunks(t):
    nc, b, h, l, d = t.shape
    return jnp.moveaxis(t, 0, 2).reshape(b, h, nc * l, d)


def forgetting_attention(q, k, v, log_f):
    b, h, s, d = q.shape
    cum = jnp.cumsum(log_f, axis=-1)
    scale = d ** -0.5
    diag = jnp.tril(jnp.ones((BLOCK, BLOCK), dtype=bool))
    outs = []
    for i in range(s // BLOCK):
        lo, hi = i * BLOCK, (i + 1) * BLOCK
        sc = jnp.einsum('bhqd,bhkd->bhqk', q[:, :, lo:hi], k[:, :, :hi]).astype(jnp.float32) * scale
        sc = sc + cum[:, :, lo:hi, None] - cum[:, :, None, :hi]
        mask = jnp.concatenate([jnp.ones((BLOCK, lo), dtype=bool), diag], axis=1)
        sc = jnp.where(mask, sc, -jnp.inf)
        p = jax.nn.softmax(sc, axis=-1).astype(v.dtype)
        outs.append(jnp.einsum('bhqk,bhkd->bhqd', p, v[:, :, :hi]))
    return jnp.concatenate(outs, axis=2)


def retention(q, k, v):
    b, h, s, d = q.shape
    nc = s // BLOCK
    q, k, v = q.astype(jnp.float32), k.astype(jnp.float32) * d ** -0.5, v.astype(jnp.float32)
    log_g = jnp.log1p(-jnp.exp2(-5.0 - jnp.arange(h, dtype=jnp.float32)))
    pos = jnp.arange(BLOCK, dtype=jnp.float32)
    rel = pos[:, None] - pos[None, :]
    intra = jnp.where(rel >= 0, jnp.exp(log_g[:, None, None] * jnp.maximum(rel, 0.0)), 0.0)
    q_dec = jnp.exp(log_g[:, None] * (pos + 1.0))[..., None]
    k_dec = jnp.exp(log_g[:, None] * (BLOCK - 1.0 - pos))[..., None]
    chunk_dec = jnp.exp(log_g * BLOCK)[:, None, None]

    def step(state, inp):
        q_c, k_c, v_c = inp
        sc = jnp.einsum('bhld,bhmd->bhlm', q_c, k_c) * intra
        o = jnp.einsum('bhlm,bhme->bhle', sc, v_c) + jnp.einsum('bhld,bhde->bhle', q_c * q_dec, state)
        state = chunk_dec * state + jnp.einsum('bhmd,bhme->bhde', k_c * k_dec, v_c)
        return state, o

    state0 = jnp.zeros((b, h, d, v.shape[-1]), jnp.float32)
    _, o = lax.scan(step, state0, (to_chunks(q, nc), to_chunks(k, nc), to_chunks(v, nc)))
    return from_chunks(o)


def mlstm(q, k, v, i_pre, f_pre):
    b, h, s, dk = q.shape
    dv = v.shape[-1]
    nc = s // BLOCK
    q, k, v = q.astype(jnp.float32), k.astype(jnp.float32) * dk ** -0.5, v.astype(jnp.float32)
    lf = jax.nn.log_sigmoid(f_pre)
    causal = jnp.tril(jnp.ones((BLOCK, BLOCK), dtype=bool))

    def step(carry, inp):
        c_st, n_st, m_st = carry
        q_c, k_c, v_c, i_c, lf_c = inp
        bcum = jnp.cumsum(lf_c, axis=-1)
        log_d = jnp.where(causal, bcum[..., :, None] - bcum[..., None, :] + i_c[..., None, :], -jnp.inf)
        log_inter = bcum + m_st[..., None]
        m_t = jnp.maximum(jnp.max(log_d, axis=-1), log_inter)
        w_intra = jnp.exp(log_d - m_t[..., None])
        w_inter = jnp.exp(log_inter - m_t)
        sc = jnp.einsum('bhld,bhmd->bhlm', q_c, k_c) * w_intra
        num = jnp.einsum('bhlm,bhme->bhle', sc, v_c) + w_inter[..., None] * jnp.einsum('bhld,bhde->bhle', q_c, c_st)
        den = jnp.sum(sc, axis=-1) + w_inter * jnp.einsum('bhld,bhd->bhl', q_c, n_st)
        h_c = num / jnp.maximum(jnp.abs(den), jnp.exp(-m_t))[..., None]
        b_last = bcum[..., -1]
        log_w = b_last[..., None] - bcum + i_c
        m_new = jnp.maximum(b_last + m_st, jnp.max(log_w, axis=-1))
        w_s = jnp.exp(log_w - m_new[..., None])
        dec = jnp.exp(b_last + m_st - m_new)
        c_st = dec[..., None, None] * c_st + jnp.einsum('bhl,bhld,bhle->bhde', w_s, k_c, v_c)
        n_st = dec[..., None] * n_st + jnp.einsum('bhl,bhld->bhd', w_s, k_c)
        return (c_st, n_st, m_new), h_c

    carry0 = (jnp.zeros((b, h, dk, dv), jnp.float32), jnp.zeros((b, h, dk), jnp.float32),
              jnp.zeros((b, h), jnp.float32))
    _, hs = lax.scan(step, carry0, (to_chunks(q, nc), to_chunks(k, nc), to_chunks(v, nc),
                                    to_chunks(i_pre, nc), to_chunks(lf, nc)))
    return from_chunks(hs)


def causal_short_conv(u, w):
    return lax.conv_general_dilated(u, w[:, None, :].astype(u.dtype), window_strides=(1,),
                                    padding=[(CONV_W - 1, 0)], dimension_numbers=('NWC', 'WIO', 'NWC'),
                                    feature_group_count=u.shape[-1])


def even_mixer(h, w_in, b_f, g_q, g_k, g_ret, w_out):
    p = h @ w_in
    qa, ka, va, fa, qb, kb, vb, gb = split_cols(p, [FOX_W, FOX_W, FOX_W, N_FOX, RET_W, RET_W, RET_W, RET_W])
    qa = rms_norm(split_heads(qa, N_FOX), g_q)
    ka = rms_norm(split_heads(ka, N_FOX), g_k)
    log_f = jax.nn.log_sigmoid(fa.astype(jnp.float32) + b_f.astype(jnp.float32)).transpose(0, 2, 1)
    ya = merge_heads(forgetting_attention(qa, ka, split_heads(va, N_FOX), log_f)).astype(h.dtype)
    yb = retention(rope(split_heads(qb, N_RET)), rope(split_heads(kb, N_RET)), split_heads(vb, N_RET))
    yb = yb - jnp.mean(yb, axis=-1, keepdims=True)
    yb = yb * lax.rsqrt(jnp.mean(yb * yb, axis=-1, keepdims=True) + EPS)
    yb = (merge_heads(yb) * g_ret.astype(jnp.float32) * jax.nn.silu(gb.astype(jnp.float32))).astype(h.dtype)
    return jnp.concatenate([ya, yb], axis=-1) @ w_out


def odd_mixer(h, w_in, conv_w, b_i, b_f, g_h, w_out):
    p = h @ w_in
    qk, v, o, ig, fg = split_cols(p, [2 * QK_W, V_W, V_W, N_MLSTM, N_MLSTM])
    qk = jax.nn.silu(causal_short_conv(qk, conv_w))
    q, k = jnp.split(qk, 2, axis=-1)
    i_pre = (ig.astype(jnp.float32) + b_i.astype(jnp.float32)).transpose(0, 2, 1)
    f_pre = (fg.astype(jnp.float32) + b_f.astype(jnp.float32)).transpose(0, 2, 1)
    hh = mlstm(split_heads(q, N_MLSTM), split_heads(k, N_MLSTM), split_heads(v, N_MLSTM), i_pre, f_pre)
    hh = hh * lax.rsqrt(jnp.mean(hh * hh, axis=-1, keepdims=True) + EPS)
    y = merge_heads(hh) * g_h.astype(jnp.float32) * jax.nn.sigmoid(o.astype(jnp.float32))
    return y.astype(h.dtype) @ w_out


def memory_cross_attention(h, mem_n, wq, wk, wv, wo, g_q, g_k):
    q = rms_norm(split_heads(h @ wq, N_XATTN), g_q)
    k = rms_norm(split_heads(mem_n @ wk, N_XATTN), g_k)
    v = split_heads(mem_n @ wv, N_XATTN)
    sc = jnp.einsum('bhsd,bhmd->bhsm', q, k).astype(jnp.float32) * XATTN_DIM ** -0.5
    p = jax.nn.softmax(sc, axis=-1).astype(v.dtype)
    return merge_heads(jnp.einsum('bhsm,bhmd->bhsd', p, v)) @ wo


def squared_relu_mlp(h, w1, w2):
    return jnp.square(jax.nn.relu(h @ w1)) @ w2


def setup_inputs(seed: int = 0) -> dict:
    key = jax.random.key(seed)
    ks = iter(jax.random.split(key, 40))
    D = D_MODEL

    def nrm(shape, scale):
        return jax.random.normal(next(ks), shape, jnp.float32) * scale

    def gain(shape):
        return 1.0 + nrm(shape, 0.02)

    return {
        "x": nrm((BATCH, SEQ, D), 1.0),
        "mem": nrm((BATCH, N_MEM, D), 1.0),
        "norm_mix": gain((DEPTH, D)),
        "norm_xattn": gain((DEPTH, D)),
        "norm_mem": gain((DEPTH, D)),
        "norm_ffn": gain((DEPTH, D)),
        "ev_w_in": nrm((N_EVEN, D, EV_IN), D ** -0.5),
        "ev_b_f": jnp.linspace(1.0, 4.0, N_FOX, dtype=jnp.float32)[None, :] + nrm((N_EVEN, N_FOX), 0.1),
        "ev_g_q": gain((N_EVEN, HEAD_DIM)),
        "ev_g_k": gain((N_EVEN, HEAD_DIM)),
        "ev_g_ret": gain((N_EVEN, RET_W)),
        "ev_w_out": nrm((N_EVEN, MIX_W, D), MIX_W ** -0.5),
        "od_w_in": nrm((N_ODD, D, OD_IN), D ** -0.5),
        "od_conv": nrm((N_ODD, CONV_W, 2 * QK_W), CONV_W ** -0.5),
        "od_b_i": nrm((N_ODD, N_MLSTM), 0.1),
        "od_b_f": jnp.linspace(3.0, 6.0, N_MLSTM, dtype=jnp.float32)[None, :] + nrm((N_ODD, N_MLSTM), 0.1),
        "od_g_h": gain((N_ODD, V_W)),
        "od_w_out": nrm((N_ODD, V_W, D), V_W ** -0.5),
        "xa_wq": nrm((DEPTH, D, D), D ** -0.5),
        "xa_wk": nrm((DEPTH, D, D), D ** -0.5),
        "xa_wv": nrm((DEPTH, D, D), D ** -0.5),
        "xa_wo": nrm((DEPTH, D, D), D ** -0.5),
        "xa_g_q": gain((DEPTH, XATTN_DIM)),
        "xa_g_k": gain((DEPTH, XATTN_DIM)),
        "ffn_w1": nrm((DEPTH, D, D_FF), D ** -0.5),
        "ffn_w2": nrm((DEPTH, D_FF, D), D_FF ** -0.5),
    }


def reference(x, mem, norm_mix, norm_xattn, norm_mem, norm_ffn,
              ev_w_in, ev_b_f, ev_g_q, ev_g_k, ev_g_ret, ev_w_out,
              od_w_in, od_conv, od_b_i, od_b_f, od_g_h, od_w_out,
              xa_wq, xa_wk, xa_wv, xa_wo, xa_g_q, xa_g_k,
              ffn_w1, ffn_w2):
    for l in range(DEPTH):
        h = rms_norm(x, norm_mix[l])
        if l % 2 == 0:
            e = l // 2
            x = x + even_mixer(h, ev_w_in[e], ev_b_f[e], ev_g_q[e], ev_g_k[e], ev_g_ret[e], ev_w_out[e])
        else:
            o = l // 2
            x = x + odd_mixer(h, od_w_in[o], od_conv[o], od_b_i[o], od_b_f[o], od_g_h[o], od_w_out[o])
        x = x + memory_cross_attention(rms_norm(x, norm_xattn[l]), rms_norm(mem, norm_mem[l]),
                                       xa_wq[l], xa_wk[l], xa_wv[l], xa_wo[l], xa_g_q[l], xa_g_k[l])
        x = x + squared_relu_mlp(rms_norm(x, norm_ffn[l]), ffn_w1[l], ffn_w2[l])
    return x
```

```python
import functools

import numpy as np
import jax
import jax.numpy as jnp
from jax import lax
from jax.experimental import pallas as pl
from jax.experimental.pallas import tpu as pltpu

F32 = jnp.float32
BF16 = jnp.bfloat16

EPS = 1e-6
LANES = 128
HEAD_DIM = 128
N_HEADS = 8
MLSTM_QK = 128
MLSTM_V = 256
N_XATTN = 4
CONV_W = 4
ROPE_BASE = 10000.0
MLSTM_CHUNK = 128
RET_CHUNK = 256
V7X_VMEM_BUDGET = 56 * 1024 * 1024


def _vmem_limit(pipelined_bytes, resident_bytes):
    return int(min(2 * pipelined_bytes + resident_bytes + (4 << 20), V7X_VMEM_BUDGET))


def _nbytes(shape, dtype):
    return int(np.prod(shape)) * jnp.dtype(dtype).itemsize


def _rms_rows(x, g):
    return x * lax.rsqrt(jnp.mean(x * x, axis=-1, keepdims=True) + EPS) * g


def _sigmoid(x):
    return 1.0 / (1.0 + jnp.exp(-x))


def _log_sigmoid(x):
    return jnp.minimum(x, 0.0) - jnp.log1p(jnp.exp(-jnp.abs(x)))


def _norm_matmul_kernel(*refs, group, n_norm_tiles, has_gate):
    x_ref, g_ref, w_ref = refs[:3]
    pos = 3
    wg_ref = eg_ref = gate_ref = None
    if has_gate:
        wg_ref = refs[pos]
        pos += 1
    if group:
        eg_ref = refs[pos]
        pos += 1
    o_ref = refs[pos]
    pos += 1
    if has_gate:
        gate_ref = refs[pos]
        pos += 1
    h_scr = refs[pos]

    j = pl.program_id(1)

    @pl.when(j == 0)
    def _():
        h_scr[...] = _rms_rows(x_ref[...], g_ref[...]).astype(BF16)
        if has_gate:
            gate_ref[...] = jnp.dot(h_scr[...], wg_ref[...], preferred_element_type=F32)

    acc = jnp.dot(h_scr[...], w_ref[...], preferred_element_type=F32)
    if not group:
        o_ref[...] = acc.astype(o_ref.dtype)
        return

    tn = acc.shape[1]

    @pl.when(j < n_norm_tiles)
    def _():
        for c in range(tn // group):
            sl = slice(c * group, (c + 1) * group)
            o_ref[:, sl] = _rms_rows(acc[:, sl], eg_ref[:, sl]).astype(o_ref.dtype)

    @pl.when(j >= n_norm_tiles)
    def _():
        o_ref[...] = acc.astype(o_ref.dtype)


def _norm_matmul(x, g, w, *, w_gate=None, group=0, group_gain=None, n_norm_cols=0, tn=512):
    t, d = x.shape
    n = w.shape[1]
    tm = min(1024, t)
    tn = min(tn, n)
    has_gate = w_gate is not None
    n_norm_tiles = n_norm_cols // tn if group else 0
    in_specs = [
        pl.BlockSpec((tm, d), lambda i, j: (i, 0)),
        pl.BlockSpec((1, d), lambda i, j: (0, 0)),
        pl.BlockSpec((d, tn), lambda i, j: (0, j)),
    ]
    args = [x, g.reshape(1, d), w]
    if has_gate:
        in_specs.append(pl.BlockSpec((d, LANES), lambda i, j: (0, 0)))
        args.append(w_gate)
    if group:
        last = n_norm_tiles - 1
        in_specs.append(pl.BlockSpec((1, tn), lambda i, j: (0, jnp.minimum(j, last))))
        args.append(group_gain.reshape(1, n_norm_cols))
    out_shape = [jax.ShapeDtypeStruct((t, n), BF16)]
    out_specs = [pl.BlockSpec((tm, tn), lambda i, j: (i, j))]
    if has_gate:
        out_shape.append(jax.ShapeDtypeStruct((t, LANES), F32))
        out_specs.append(pl.BlockSpec((tm, LANES), lambda i, j: (i, 0)))
    pipelined = (_nbytes((tm, d), F32) + _nbytes((d, tn), BF16) + _nbytes((tm, tn), BF16)
                 + _nbytes((tm, LANES), F32) + _nbytes((d, LANES), BF16))
    resident = _nbytes((tm, d), BF16) + 3 * _nbytes((tm, tn), F32) + _nbytes((tm, d), F32)
    outs = pl.pallas_call(
        functools.partial(_norm_matmul_kernel, group=group, n_norm_tiles=n_norm_tiles, has_gate=has_gate),
        grid=(t // tm, n // tn),
        in_specs=in_specs,
        out_specs=out_specs,
        out_shape=out_shape,
        scratch_shapes=[pltpu.VMEM((tm, d), BF16)],
        compiler_params=pltpu.CompilerParams(
            dimension_semantics=("parallel", "arbitrary"),
            vmem_limit_bytes=_vmem_limit(pipelined, resident)),
        name="norm_matmul",
    )(*args)
    return outs if has_gate else outs[0]


def _out_proj_kernel(al_ref, ar_ref, w_ref, x_ref, o_ref):
    kl = al_ref.shape[1]
    acc = jnp.dot(al_ref[...], w_ref[:kl, :], preferred_element_type=F32)
    acc += jnp.dot(ar_ref[...], w_ref[kl:, :], preferred_element_type=F32)
    o_ref[...] = x_ref[...] + acc


def _out_proj(a_left, col_left, a_right, col_right, w, x):
    t, d = x.shape
    k = w.shape[0]
    kh = k // 2
    tm = min(512, t)
    pipelined = 2 * _nbytes((tm, kh), BF16) + 2 * _nbytes((tm, d), F32) + _nbytes((k, d), BF16)
    return pl.pallas_call(
        _out_proj_kernel,
        grid=(t // tm,),
        in_specs=[
            pl.BlockSpec((tm, kh), lambda i: (i, col_left)),
            pl.BlockSpec((tm, kh), lambda i: (i, col_right)),
            pl.BlockSpec((k, d), lambda i: (0, 0)),
            pl.BlockSpec((tm, d), lambda i: (i, 0)),
        ],
        out_specs=pl.BlockSpec((tm, d), lambda i: (i, 0)),
        out_shape=jax.ShapeDtypeStruct((t, d), F32),
        compiler_params=pltpu.CompilerParams(
            dimension_semantics=("parallel",),
            vmem_limit_bytes=_vmem_limit(pipelined, _nbytes((tm, d), F32))),
        name="out_proj",
    )(a_left, a_right, w, x)


def _mlp_kernel(x_ref, g_ref, w1_ref, w2_ref, o_ref, h_scr):
    @pl.when(pl.program_id(1) == 0)
    def _():
        x = x_ref[...]
        h_scr[...] = _rms_rows(x, g_ref[...]).astype(BF16)
        o_ref[...] = x

    u = jnp.maximum(jnp.dot(h_scr[...], w1_ref[...], preferred_element_type=F32), 0.0)
    o_ref[...] += jnp.dot((u * u).astype(BF16), w2_ref[...], preferred_element_type=F32)


def _mlp(x, g, w1, w2):
    t, d = x.shape
    f = w1.shape[1]
    tm = min(1024, t)
    tf = min(512, f)
    pipelined = 2 * _nbytes((tm, d), F32) + 2 * _nbytes((d, tf), BF16)
    resident = _nbytes((tm, d), BF16) + 2 * _nbytes((tm, tf), F32)
    return pl.pallas_call(
        _mlp_kernel,
        grid=(t // tm, f // tf),
        in_specs=[
            pl.BlockSpec((tm, d), lambda i, j: (i, 0)),
            pl.BlockSpec((1, d), lambda i, j: (0, 0)),
            pl.BlockSpec((d, tf), lambda i, j: (0, j)),
            pl.BlockSpec((tf, d), lambda i, j: (j, 0)),
        ],
        out_specs=pl.BlockSpec((tm, d), lambda i, j: (i, 0)),
        out_shape=jax.ShapeDtypeStruct((t, d), F32),
        scratch_shapes=[pltpu.VMEM((tm, d), BF16)],
        compiler_params=pltpu.CompilerParams(
            dimension_semantics=("parallel", "arbitrary"),
            vmem_limit_bytes=_vmem_limit(pipelined, resident)),
        name="mlp",
    )(x, g.reshape(1, d), w1, w2)


def _upper_tri_ones():
    r = lax.broadcasted_iota(jnp.int32, (LANES, LANES), 0)
    c = lax.broadcasted_iota(jnp.int32, (LANES, LANES), 1)
    return jnp.where(r <= c, 1.0, 0.0).astype(BF16)


def _cumsum_lanes(x, tri):
    hi = x.astype(BF16)
    r1 = x - hi.astype(F32)
    mid = r1.astype(BF16)
    lo = (r1 - mid.astype(F32)).astype(BF16)
    out = jnp.dot(hi, tri, preferred_element_type=F32)
    out += jnp.dot(mid, tri, preferred_element_type=F32)
    out += jnp.dot(lo, tri, preferred_element_type=F32)
    return out


def _fox_gate_kernel(f_ref, b_ref, o_ref, carry_scr):
    @pl.when(pl.program_id(1) == 0)
    def _():
        carry_scr[...] = jnp.zeros_like(carry_scr)

    tri = _upper_tri_ones()
    carry = carry_scr[:, 0:1]
    for c in range(f_ref.shape[2] // LANES):
        sl = slice(c * LANES, (c + 1) * LANES)
        ls = _log_sigmoid(f_ref[0, :, sl] + b_ref[...])
        cum = _cumsum_lanes(ls, tri) + carry
        o_ref[0, :, sl] = cum
        carry = cum[:, LANES - 1:LANES]
    carry_scr[...] = jnp.broadcast_to(carry, carry_scr.shape)


def _fox_gates(f_rows, b_f):
    bsz, h, s = f_rows.shape
    lb = min(2048, s)
    return pl.pallas_call(
        _fox_gate_kernel,
        grid=(bsz, s // lb),
        in_specs=[pl.BlockSpec((1, h, lb), lambda b, i: (b, 0, i)),
                  pl.BlockSpec((h, 1), lambda b, i: (0, 0))],
        out_specs=pl.BlockSpec((1, h, lb), lambda b, i: (b, 0, i)),
        out_shape=jax.ShapeDtypeStruct((bsz, h, s), F32),
        scratch_shapes=[pltpu.VMEM((h, LANES), F32)],
        compiler_params=pltpu.CompilerParams(dimension_semantics=("parallel", "arbitrary")),
        name="fox_gates",
    )(f_rows, b_f.reshape(h, 1))


def _mlstm_gate_kernel(i_ref, f_ref, bi_ref, bf_ref, io_ref, bo_ref):
    tri = _upper_tri_ones()
    io_ref[0] = i_ref[0] + bi_ref[...]
    for c in range(f_ref.shape[2] // LANES):
        sl = slice(c * LANES, (c + 1) * LANES)
        bo_ref[0, :, sl] = _cumsum_lanes(_log_sigmoid(f_ref[0, :, sl] + bf_ref[...]), tri)


def _mlstm_gates(i_rows, f_rows, b_i, b_f):
    assert MLSTM_CHUNK == LANES
    bsz, h, s = f_rows.shape
    lb = min(2048, s)
    row = pl.BlockSpec((1, h, lb), lambda b, i: (b, 0, i))
    bias = pl.BlockSpec((h, 1), lambda b, i: (0, 0))
    return pl.pallas_call(
        _mlstm_gate_kernel,
        grid=(bsz, s // lb),
        in_specs=[row, row, bias, bias],
        out_specs=[row, row],
        out_shape=[jax.ShapeDtypeStruct((bsz, h, s), F32)] * 2,
        compiler_params=pltpu.CompilerParams(dimension_semantics=("parallel", "parallel")),
        name="mlstm_gates",
    )(i_rows, f_rows, b_i.reshape(h, 1), b_f.reshape(h, 1))


def _fox_kernel(q_ref, k_ref, v_ref, c_ref, o_ref, *, blk):
    qi = pl.program_id(2)
    q = q_ref[...]
    c0 = c_ref[0, qi][:, 0:1]

    def step(j, carry, masked):
        m, l, acc = carry
        off = pl.multiple_of(j * blk, blk)
        k = k_ref[pl.ds(off, blk), :]
        v = v_ref[pl.ds(off, blk), :]
        s = lax.dot_general(q, k, (((1,), (1,)), ((), ())), preferred_element_type=F32)
        s = s + (c0 - c_ref[0, j])
        if masked:
            row = lax.broadcasted_iota(jnp.int32, s.shape, 0)
            col = lax.broadcasted_iota(jnp.int32, s.shape, 1)
            s = jnp.where(row >= col, s, -1e30)
        m_new = jnp.maximum(m, jnp.max(s, axis=-1, keepdims=True))
        alpha = jnp.exp(m - m_new)
        p = jnp.exp(s - m_new)
        l = alpha * l + jnp.sum(p, axis=-1, keepdims=True)
        acc = alpha * acc + jnp.dot(p.astype(BF16), v, preferred_element_type=F32)
        return m_new, l, acc

    init = (jnp.full((blk, 1), -1e30, F32), jnp.zeros((blk, 1), F32), jnp.zeros((blk, HEAD_DIM), F32))
    carry = lax.fori_loop(0, qi, lambda j, c: step(j, c, False), init)
    _, l, acc = step(qi, carry, True)
    o_ref[...] = (acc / l).astype(o_ref.dtype)


def _fox_attention(proj, cum, bsz, s):
    t = proj.shape[0]
    blk = min(512, s)
    nb = s // blk
    cum4 = cum.reshape(bsz * N_HEADS, nb, 1, blk)
    pipelined = 2 * _nbytes((s, HEAD_DIM), BF16) + 2 * _nbytes((blk, HEAD_DIM), BF16) + _nbytes((nb, 8, blk), F32)
    resident = 6 * _nbytes((blk, blk), F32)
    return pl.pallas_call(
        functools.partial(_fox_kernel, blk=blk),
        grid=(bsz, N_HEADS, nb),
        in_specs=[
            pl.BlockSpec((blk, HEAD_DIM), lambda b, h, i: (b * nb + i, h)),
            pl.BlockSpec((s, HEAD_DIM), lambda b, h, i: (b, N_HEADS + h)),
            pl.BlockSpec((s, HEAD_DIM), lambda b, h, i: (b, 2 * N_HEADS + h)),
            pl.BlockSpec((1, nb, 1, blk), lambda b, h, i: (b * N_HEADS + h, 0, 0, 0)),
        ],
        out_specs=pl.BlockSpec((blk, HEAD_DIM), lambda b, h, i: (b * nb + i, h)),
        out_shape=jax.ShapeDtypeStruct((t, N_HEADS * HEAD_DIM), BF16),
        compiler_params=pltpu.CompilerParams(
            dimension_semantics=("parallel", "parallel", "arbitrary"),
            vmem_limit_bytes=_vmem_limit(pipelined, resident)),
        name="fox_attention",
    )(proj, proj, proj, cum4)


def _retention_tables(chunk):
    h = np.arange(N_HEADS, dtype=np.float64)
    log_g = np.log1p(-np.exp2(-5.0 - h))
    pos = np.arange(chunk, dtype=np.float64)
    rel = pos[:, None] - pos[None, :]
    intra = np.where(rel >= 0, np.exp(log_g[:, None, None] * np.maximum(rel, 0.0)), 0.0)
    q_dec = np.exp(log_g[:, None] * (pos + 1.0))[..., None] * np.ones((1, 1, LANES))
    k_dec = np.exp(log_g[:, None] * (chunk - 1.0 - pos))[..., None] * np.ones((1, 1, LANES))
    c_dec = np.exp(log_g * chunk)[:, None, None] * np.ones((1, 1, LANES))
    return tuple(jnp.asarray(a, F32) for a in (intra, q_dec, k_dec, c_dec))


def _rope(t, cos, sin):
    return t * cos + pltpu.roll(t, HEAD_DIM // 2, 1) * sin


def _retention_kernel(q_ref, k_ref, v_ref, g_ref, cos_ref, sin_ref, intra_ref, qd_ref, kd_ref, cd_ref,
                      gr_ref, o_ref, state_scr, *, chunk):
    @pl.when(pl.program_id(2) == 0)
    def _():
        state_scr[...] = jnp.zeros_like(state_scr)

    intra, q_dec, k_dec, c_dec = intra_ref[0], qd_ref[0], kd_ref[0], cd_ref[0]
    state = state_scr[...]
    for c in range(q_ref.shape[0] // chunk):
        rows = slice(c * chunk, (c + 1) * chunk)
        cos, sin = cos_ref[rows, :], sin_ref[rows, :]
        q = _rope(q_ref[rows, :].astype(F32), cos, sin)
        k = _rope(k_ref[rows, :].astype(F32), cos, sin) * (HEAD_DIM ** -0.5)
        v = v_ref[rows, :]
        sc = lax.dot_general(q.astype(BF16), k.astype(BF16), (((1,), (1,)), ((), ())),
                             preferred_element_type=F32) * intra
        o = jnp.dot(sc.astype(BF16), v, preferred_element_type=F32)
        o += jnp.dot((q * q_dec).astype(BF16), state.astype(BF16), preferred_element_type=F32)
        state = c_dec * state + lax.dot_general((k * k_dec).astype(BF16), v, (((0,), (0,)), ((), ())),
                                                preferred_element_type=F32)
        y = o - jnp.mean(o, axis=-1, keepdims=True)
        y = y * lax.rsqrt(jnp.mean(y * y, axis=-1, keepdims=True) + EPS)
        gate = g_ref[rows, :].astype(F32)
        o_ref[rows, :] = (y * gr_ref[...] * (gate * _sigmoid(gate))).astype(o_ref.dtype)
    state_scr[...] = state


def _retention(proj, col0, g_ret, cos_t, sin_t, bsz, s):
    t = proj.shape[0]
    chunk = min(RET_CHUNK, s)
    lb = min(1024, s)
    nb = s // lb
    intra, q_dec, k_dec, c_dec = _retention_tables(chunk)

    def col(group):
        return pl.BlockSpec((lb, HEAD_DIM), lambda b, h, i: (b * nb + i, col0 + group * N_HEADS + h))

    def table(shape):
        return pl.BlockSpec((1,) + shape, lambda b, h, i: (h, 0, 0))

    pipelined = 5 * _nbytes((lb, HEAD_DIM), BF16) + 2 * _nbytes((lb, HEAD_DIM), F32)
    resident = 2 * (_nbytes((chunk, chunk), F32) + 2 * _nbytes((chunk, LANES), F32)) + 8 * _nbytes((chunk, chunk), F32)
    return pl.pallas_call(
        functools.partial(_retention_kernel, chunk=chunk),
        grid=(bsz, N_HEADS, nb),
        in_specs=[
            col(0), col(1), col(2), col(3),
            pl.BlockSpec((lb, HEAD_DIM), lambda b, h, i: (i, 0)),
            pl.BlockSpec((lb, HEAD_DIM), lambda b, h, i: (i, 0)),
            table((chunk, chunk)), table((chunk, LANES)), table((chunk, LANES)), table((1, LANES)),
            pl.BlockSpec((1, HEAD_DIM), lambda b, h, i: (0, h)),
        ],
        out_specs=pl.BlockSpec((lb, HEAD_DIM), lambda b, h, i: (b * nb + i, h)),
        out_shape=jax.ShapeDtypeStruct((t, N_HEADS * HEAD_DIM), BF16),
        scratch_shapes=[pltpu.VMEM((HEAD_DIM, HEAD_DIM), F32)],
        compiler_params=pltpu.CompilerParams(
            dimension_semantics=("parallel", "parallel", "arbitrary"),
            vmem_limit_bytes=_vmem_limit(pipelined, resident)),
        name="retention",
    )(proj, proj, proj, proj, cos_t, sin_t, intra, q_dec, k_dec, c_dec, g_ret.reshape(1, -1))


def _conv_silu(cur, tail, w):
    n = cur.shape[0]
    row8 = lax.broadcasted_iota(jnp.int32, (8, cur.shape[1]), 0)
    out = cur * w[CONV_W - 1:CONV_W, :]
    for shift in range(1, CONV_W):
        rolled = pltpu.roll(cur, shift, 0)
        head = jnp.where(row8 < shift, pltpu.roll(tail, shift, 0), rolled[:8, :])
        shifted = jnp.concatenate([head, rolled[8:, :]], axis=0)
        out += shifted * w[CONV_W - 1 - shift:CONV_W - shift, :]
    del n
    return out * _sigmoid(out)


def _mlstm_kernel(q_ref, k_ref, v_ref, og_ref, wq_ref, wk_ref, i_ref, b_ref, gh_ref, o_ref,
                  c_scr, n_scr, m_scr, qt_scr, kt_scr, *, chunk):
    @pl.when(pl.program_id(2) == 0)
    def _():
        c_scr[...] = jnp.zeros_like(c_scr)
        n_scr[...] = jnp.zeros_like(n_scr)
        m_scr[...] = jnp.zeros_like(m_scr)
        qt_scr[...] = jnp.zeros_like(qt_scr)
        kt_scr[...] = jnp.zeros_like(kt_scr)

    row = lax.broadcasted_iota(jnp.int32, (chunk, chunk), 0)
    col = lax.broadcasted_iota(jnp.int32, (chunk, chunk), 1)
    eye = row == col
    causal = col <= row

    def to_col(r):
        return jnp.sum(jnp.where(eye, r, 0.0), axis=1, keepdims=True)

    c_st, n_st, m_st = c_scr[...], n_scr[0:1, :], m_scr[0:1, 0:1]
    q_tail, k_tail = qt_scr[...], kt_scr[...]
    wq, wk = wq_ref[...], wk_ref[...]
    for c in range(q_ref.shape[0] // chunk):
        rows = slice(c * chunk, (c + 1) * chunk)
        q_pre = q_ref[rows, :].astype(F32)
        k_pre = k_ref[rows, :].astype(F32)
        q = _conv_silu(q_pre, q_tail, wq)
        k = _conv_silu(k_pre, k_tail, wk) * (MLSTM_QK ** -0.5)
        q_tail, k_tail = q_pre[chunk - 8:, :], k_pre[chunk - 8:, :]
        v = v_ref[rows, :]
        i_row = i_ref[0, :, rows]
        b_row = b_ref[0, :, rows]
        b_col = to_col(b_row)
        a_row = i_row - b_row
        log_d = jnp.where(causal, b_col + a_row, -jnp.inf)
        m_t = jnp.maximum(jnp.max(log_d, axis=1, keepdims=True), b_col + m_st)
        w_intra = jnp.exp(log_d - m_t)
        w_inter = jnp.exp(b_col + m_st - m_t)
        qb = q.astype(BF16)
        sc = lax.dot_general(qb, k.astype(BF16), (((1,), (1,)), ((), ())), preferred_element_type=F32) * w_intra
        num = jnp.dot(sc.astype(BF16), v, preferred_element_type=F32)
        num += w_inter * jnp.dot(qb, c_st.astype(BF16), preferred_element_type=F32)
        den = jnp.sum(sc, axis=1, keepdims=True) + w_inter * jnp.sum(q * n_st, axis=1, keepdims=True)
        hid = num / jnp.maximum(jnp.abs(den), jnp.exp(-m_t))
        b_last = b_row[:, chunk - 1:chunk]
        log_w = b_last + a_row
        m_new = jnp.maximum(b_last + m_st, jnp.max(log_w, axis=1, keepdims=True))
        kw = k * to_col(jnp.exp(log_w - m_new))
        dec = jnp.exp(b_last + m_st - m_new)
        c_st = dec * c_st + lax.dot_general(kw.astype(BF16), v, (((0,), (0,)), ((), ())),
                                            preferred_element_type=F32)
        n_st = dec * n_st + jnp.sum(kw, axis=0, keepdims=True)
        m_st = m_new
        hid = hid * lax.rsqrt(jnp.mean(hid * hid, axis=-1, keepdims=True) + EPS)
        o_ref[rows, :] = (hid * gh_ref[...] * _sigmoid(og_ref[rows, :].astype(F32))).astype(o_ref.dtype)
    c_scr[...] = c_st
    n_scr[...] = jnp.broadcast_to(n_st, n_scr.shape)
    m_scr[...] = jnp.broadcast_to(m_st, m_scr.shape)
    qt_scr[...] = q_tail
    kt_scr[...] = k_tail


def _mlstm(proj, conv_w, i_rows, bcum_rows, g_h, bsz, s):
    t = proj.shape[0]
    chunk = MLSTM_CHUNK
    lb = min(512, s)
    nb = s // lb
    vcol = 2 * N_HEADS * MLSTM_QK // MLSTM_V
    gates = pl.BlockSpec((1, 1, lb), lambda b, h, i: (b * N_HEADS + h, 0, i))
    pipelined = 2 * _nbytes((lb, MLSTM_QK), BF16) + 3 * _nbytes((lb, MLSTM_V), BF16)
    resident = 2 * _nbytes((MLSTM_QK, MLSTM_V), F32) + 10 * _nbytes((chunk, MLSTM_V), F32)
    return pl.pallas_call(
        functools.partial(_mlstm_kernel, chunk=chunk),
        grid=(bsz, N_HEADS, nb),
        in_specs=[
            pl.BlockSpec((lb, MLSTM_QK), lambda b, h, i: (b * nb + i, h)),
            pl.BlockSpec((lb, MLSTM_QK), lambda b, h, i: (b * nb + i, N_HEADS + h)),
            pl.BlockSpec((lb, MLSTM_V), lambda b, h, i: (b * nb + i, vcol + h)),
            pl.BlockSpec((lb, MLSTM_V), lambda b, h, i: (b * nb + i, vcol + N_HEADS + h)),
            pl.BlockSpec((CONV_W, MLSTM_QK), lambda b, h, i: (0, h)),
            pl.BlockSpec((CONV_W, MLSTM_QK), lambda b, h, i: (0, N_HEADS + h)),
            gates, gates,
            pl.BlockSpec((1, MLSTM_V), lambda b, h, i: (0, h)),
        ],
        out_specs=pl.BlockSpec((lb, MLSTM_V), lambda b, h, i: (b * nb + i, h)),
        out_shape=jax.ShapeDtypeStruct((t, N_HEADS * MLSTM_V), BF16),
        scratch_shapes=[
            pltpu.VMEM((MLSTM_QK, MLSTM_V), F32),
            pltpu.VMEM((8, MLSTM_QK), F32),
            pltpu.VMEM((8, LANES), F32),
            pltpu.VMEM((8, MLSTM_QK), F32),
            pltpu.VMEM((8, MLSTM_QK), F32),
        ],
        compiler_params=pltpu.CompilerParams(
            dimension_semantics=("parallel", "parallel", "arbitrary"),
            vmem_limit_bytes=_vmem_limit(pipelined, resident)),
        name="mlstm",
    )(proj, proj, proj, proj, conv_w, conv_w,
      i_rows.reshape(bsz * N_HEADS, 1, s), bcum_rows.reshape(bsz * N_HEADS, 1, s), g_h.reshape(1, -1))


def _xattn_kernel(q_ref, k_ref, v_ref, o_ref):
    dh = q_ref.shape[1] // N_XATTN
    for h in range(N_XATTN):
        sl = slice(h * dh, (h + 1) * dh)
        s = lax.dot_general(q_ref[:, sl], k_ref[:, sl], (((1,), (1,)), ((), ())), preferred_element_type=F32)
        p = jnp.exp(s - jnp.max(s, axis=-1, keepdims=True))
        p = p / jnp.sum(p, axis=-1, keepdims=True)
        o_ref[:, sl] = jnp.dot(p.astype(BF16), v_ref[:, sl], preferred_element_type=F32).astype(o_ref.dtype)


def _xattn(q, kv, s, n_mem):
    t, d = q.shape
    tm = min(512, s)
    per_seq = s // tm
    pipelined = 2 * _nbytes((tm, d), BF16) + 2 * _nbytes((n_mem, d), BF16)
    resident = 4 * _nbytes((tm, n_mem), F32) + _nbytes((tm, d), F32)
    return pl.pallas_call(
        _xattn_kernel,
        grid=(t // tm,),
        in_specs=[
            pl.BlockSpec((tm, d), lambda i: (i, 0)),
            pl.BlockSpec((n_mem, d), lambda i: (i // per_seq, 0)),
            pl.BlockSpec((n_mem, d), lambda i: (i // per_seq, 1)),
        ],
        out_specs=pl.BlockSpec((tm, d), lambda i: (i, 0)),
        out_shape=jax.ShapeDtypeStruct((t, d), BF16),
        compiler_params=pltpu.CompilerParams(
            dimension_semantics=("parallel",),
            vmem_limit_bytes=_vmem_limit(pipelined, resident)),
        name="xattn",
    )(q, kv, kv)


def _gate_weight(w_cols):
    return jnp.pad(w_cols, ((0, 0), (0, LANES - w_cols.shape[1]))).astype(BF16)


def _rows(gate_cols, bsz, s):
    return gate_cols.reshape(bsz, s, -1).transpose(0, 2, 1)


def _rope_tables(s):
    inv = 1.0 / (ROPE_BASE ** (jnp.arange(0, HEAD_DIM, 2, dtype=F32) / HEAD_DIM))
    ang = jnp.arange(s, dtype=F32)[:, None] * inv[None, :]
    cos, sin = jnp.cos(ang), jnp.sin(ang)
    return jnp.concatenate([cos, cos], axis=-1), jnp.concatenate([-sin, sin], axis=-1)


def _even_mixer(x, g, w_in, b_f, g_q, g_k, g_ret, w_out, rope, bsz, s):
    fox_w = N_HEADS * HEAD_DIM
    w_main = jnp.concatenate([w_in[:, :3 * fox_w], w_in[:, 3 * fox_w + N_HEADS:]], axis=1).astype(BF16)
    w_gate = _gate_weight(w_in[:, 3 * fox_w:3 * fox_w + N_HEADS])
    qk_gain = jnp.concatenate([jnp.tile(g_q * HEAD_DIM ** -0.5, N_HEADS), jnp.tile(g_k, N_HEADS)])
    proj, gate = _norm_matmul(x, g, w_main, w_gate=w_gate, group=HEAD_DIM, group_gain=qk_gain,
                              n_norm_cols=2 * fox_w)
    cum = _fox_gates(_rows(gate[:, :N_HEADS], bsz, s), b_f)
    ya = _fox_attention(proj, cum, bsz, s)
    yb = _retention(proj, 3 * N_HEADS, g_ret, rope[0], rope[1], bsz, s)
    return _out_proj(ya, 0, yb, 0, w_out.astype(BF16), x)


def _odd_mixer(x, g, w_in, conv_w, b_i, b_f, g_h, w_out, bsz, s):
    n_main = 2 * N_HEADS * MLSTM_QK + 2 * N_HEADS * MLSTM_V
    proj, gate = _norm_matmul(x, g, w_in[:, :n_main].astype(BF16), w_gate=_gate_weight(w_in[:, n_main:]))
    i_rows, bcum_rows = _mlstm_gates(_rows(gate[:, :N_HEADS], bsz, s),
                                     _rows(gate[:, N_HEADS:2 * N_HEADS], bsz, s), b_i, b_f)
    y = _mlstm(proj, conv_w, i_rows, bcum_rows, g_h, bsz, s)
    return _out_proj(y, 0, y, 1, w_out.astype(BF16), x)


def _cross_attention(x, mem, g_x, g_m, wq, wk, wv, wo, g_q, g_k, s):
    d = x.shape[1]
    dh = d // N_XATTN
    n_mem = mem.shape[0] * s // x.shape[0]
    kv = _norm_matmul(mem, g_m, jnp.concatenate([wk, wv], axis=1).astype(BF16),
                      group=dh, group_gain=jnp.tile(g_k, N_XATTN), n_norm_cols=d)
    q = _norm_matmul(x, g_x, wq.astype(BF16), group=dh, group_gain=jnp.tile(g_q * dh ** -0.5, N_XATTN),
                     n_norm_cols=d)
    o = _xattn(q, kv, s, n_mem)
    return _out_proj(o, 0, o, 1, wo.astype(BF16), x)


def kernel(x, mem, norm_mix, norm_xattn, norm_mem, norm_ffn, ev_w_in, ev_b_f, ev_g_q, ev_g_k, ev_g_ret, ev_w_out, od_w_in, od_conv, od_b_i, od_b_f, od_g_h, od_w_out, xa_wq, xa_wk, xa_wv, xa_wo, xa_g_q, xa_g_k, ffn_w1, ffn_w2):
    bsz, s, d = x.shape
    depth = norm_mix.shape[0]
    xf = x.reshape(bsz * s, d)
    memf = mem.reshape(-1, d)
    rope = _rope_tables(s)
    for l in range(depth):
        if l % 2 == 0:
            e = l // 2
            xf = _even_mixer(xf, norm_mix[l], ev_w_in[e], ev_b_f[e], ev_g_q[e], ev_g_k[e], ev_g_ret[e],
                             ev_w_out[e], rope, bsz, s)
        else:
            o = l // 2
            xf = _odd_mixer(xf, norm_mix[l], od_w_in[o], od_conv[o], od_b_i[o], od_b_f[o], od_g_h[o],
                            od_w_out[o], bsz, s)
        xf = _cross_attention(xf, memf, norm_xattn[l], norm_mem[l], xa_wq[l], xa_wk[l], xa_wv[l], xa_wo[l],
                              xa_g_q[l], xa_g_k[l], s)
        xf = _mlp(xf, norm_ffn[l], ffn_w1[l].astype(BF16), ffn_w2[l].astype(BF16))
    return xf.reshape(bsz, s, d)
```

```python
import functools

import numpy as np
import jax
import jax.numpy as jnp
from jax import lax
from jax.experimental import pallas as pl
from jax.experimental.pallas import tpu as pltpu

F32 = jnp.float32
BF16 = jnp.bfloat16

EPS = 1e-6
LOG2E = 1.4426950408889634
LANES = 128
BF16_SUBLANES = 16
HEAD_DIM = 128
N_HEADS = 8
MLSTM_QK = 128
MLSTM_V = 256
N_XATTN = 4
CONV_W = 4
ROPE_BASE = 10000.0
MLSTM_CHUNK = 128
RET_CHUNK = 256
V7X_VMEM_BUDGET = 56 * 1024 * 1024


def _vmem_limit(pipelined_bytes, resident_bytes):
    return int(min(2 * pipelined_bytes + resident_bytes + (4 << 20), V7X_VMEM_BUDGET))


def _nbytes(shape, dtype):
    return int(np.prod(shape)) * jnp.dtype(dtype).itemsize


def _rms_rows(x, g):
    return x * lax.rsqrt(jnp.mean(x * x, axis=-1, keepdims=True) + EPS) * g


def _sigmoid(x):
    return 1.0 / (1.0 + jnp.exp(-x))


def _log_sigmoid(x):
    return jnp.minimum(x, 0.0) - jnp.log1p(jnp.exp(-jnp.abs(x)))


def _norm_matmul_kernel(*refs, group, n_norm_tiles, has_gate):
    x_ref, g_ref, w_ref = refs[:3]
    pos = 3
    wg_ref = eg_ref = gate_ref = None
    if has_gate:
        wg_ref = refs[pos]
        pos += 1
    if group:
        eg_ref = refs[pos]
        pos += 1
    o_ref = refs[pos]
    pos += 1
    if has_gate:
        gate_ref = refs[pos]
        pos += 1
    h_scr = refs[pos]

    j = pl.program_id(1)

    @pl.when(j == 0)
    def _():
        h_scr[...] = _rms_rows(x_ref[...], g_ref[...]).astype(BF16)
        if has_gate:
            gate_ref[...] = jnp.dot(h_scr[...], wg_ref[...], preferred_element_type=F32)

    acc = jnp.dot(h_scr[...], w_ref[...], preferred_element_type=F32)
    if not group:
        o_ref[...] = acc.astype(o_ref.dtype)
        return

    tn = acc.shape[1]

    @pl.when(j < n_norm_tiles)
    def _():
        for c in range(tn // group):
            sl = slice(c * group, (c + 1) * group)
            o_ref[:, sl] = _rms_rows(acc[:, sl], eg_ref[:, sl]).astype(o_ref.dtype)

    @pl.when(j >= n_norm_tiles)
    def _():
        o_ref[...] = acc.astype(o_ref.dtype)


def _norm_matmul(x, g, w, *, w_gate=None, group=0, group_gain=None, n_norm_cols=0, tn=1024):
    t, d = x.shape
    n = w.shape[1]
    tm = min(1024, t)
    tn = min(tn, n)
    has_gate = w_gate is not None
    n_norm_tiles = n_norm_cols // tn if group else 0
    in_specs = [
        pl.BlockSpec((tm, d), lambda i, j: (i, 0)),
        pl.BlockSpec((1, d), lambda i, j: (0, 0)),
        pl.BlockSpec((d, tn), lambda i, j: (0, j)),
    ]
    args = [x, g.reshape(1, d), w]
    if has_gate:
        in_specs.append(pl.BlockSpec((d, LANES), lambda i, j: (0, 0)))
        args.append(w_gate)
    if group:
        last = n_norm_tiles - 1
        in_specs.append(pl.BlockSpec((1, tn), lambda i, j: (0, jnp.minimum(j, last))))
        args.append(group_gain.reshape(1, n_norm_cols))
    out_shape = [jax.ShapeDtypeStruct((t, n), BF16)]
    out_specs = [pl.BlockSpec((tm, tn), lambda i, j: (i, j))]
    if has_gate:
        out_shape.append(jax.ShapeDtypeStruct((t, LANES), F32))
        out_specs.append(pl.BlockSpec((tm, LANES), lambda i, j: (i, 0)))
    pipelined = (_nbytes((tm, d), F32) + _nbytes((d, tn), BF16) + _nbytes((tm, tn), BF16)
                 + _nbytes((tm, LANES), F32) + _nbytes((d, LANES), BF16))
    resident = _nbytes((tm, d), BF16) + 3 * _nbytes((tm, tn), F32) + _nbytes((tm, d), F32)
    outs = pl.pallas_call(
        functools.partial(_norm_matmul_kernel, group=group, n_norm_tiles=n_norm_tiles, has_gate=has_gate),
        grid=(t // tm, n // tn),
        in_specs=in_specs,
        out_specs=out_specs,
        out_shape=out_shape,
        scratch_shapes=[pltpu.VMEM((tm, d), BF16)],
        compiler_params=pltpu.CompilerParams(
            dimension_semantics=("parallel", "arbitrary"),
            vmem_limit_bytes=_vmem_limit(pipelined, resident)),
        name="norm_matmul",
    )(*args)
    return outs if has_gate else outs[0]


def _out_proj_kernel(al_ref, ar_ref, w_ref, x_ref, o_ref):
    kl = al_ref.shape[1]
    acc = jnp.dot(al_ref[...], w_ref[:kl, :], preferred_element_type=F32)
    acc += jnp.dot(ar_ref[...], w_ref[kl:, :], preferred_element_type=F32)
    o_ref[...] = x_ref[...] + acc


def _out_proj(a_left, col_left, a_right, col_right, w, x):
    t, d = x.shape
    k = w.shape[0]
    kh = k // 2
    tm = min(512, t)
    pipelined = 2 * _nbytes((tm, kh), BF16) + 2 * _nbytes((tm, d), F32) + _nbytes((k, d), BF16)
    return pl.pallas_call(
        _out_proj_kernel,
        grid=(t // tm,),
        in_specs=[
            pl.BlockSpec((tm, kh), lambda i: (i, col_left)),
            pl.BlockSpec((tm, kh), lambda i: (i, col_right)),
            pl.BlockSpec((k, d), lambda i: (0, 0)),
            pl.BlockSpec((tm, d), lambda i: (i, 0)),
        ],
        out_specs=pl.BlockSpec((tm, d), lambda i: (i, 0)),
        out_shape=jax.ShapeDtypeStruct((t, d), F32),
        compiler_params=pltpu.CompilerParams(
            dimension_semantics=("parallel",),
            vmem_limit_bytes=_vmem_limit(pipelined, _nbytes((tm, d), F32))),
        name="out_proj",
    )(a_left, a_right, w, x)


def _mlp_kernel(x_ref, g_ref, w1_ref, w2_ref, o_ref, h_scr):
    @pl.when(pl.program_id(1) == 0)
    def _():
        x = x_ref[...]
        h_scr[...] = _rms_rows(x, g_ref[...]).astype(BF16)
        o_ref[...] = x

    u = jnp.maximum(jnp.dot(h_scr[...], w1_ref[...], preferred_element_type=F32), 0.0)
    o_ref[...] += jnp.dot((u * u).astype(BF16), w2_ref[...], preferred_element_type=F32)


def _mlp(x, g, w1, w2):
    t, d = x.shape
    f = w1.shape[1]
    tm = min(1024, t)
    tf = min(512, f)
    pipelined = 2 * _nbytes((tm, d), F32) + 2 * _nbytes((d, tf), BF16)
    resident = _nbytes((tm, d), BF16) + 2 * _nbytes((tm, tf), F32)
    return pl.pallas_call(
        _mlp_kernel,
        grid=(t // tm, f // tf),
        in_specs=[
            pl.BlockSpec((tm, d), lambda i, j: (i, 0)),
            pl.BlockSpec((1, d), lambda i, j: (0, 0)),
            pl.BlockSpec((d, tf), lambda i, j: (0, j)),
            pl.BlockSpec((tf, d), lambda i, j: (j, 0)),
        ],
        out_specs=pl.BlockSpec((tm, d), lambda i, j: (i, 0)),
        out_shape=jax.ShapeDtypeStruct((t, d), F32),
        scratch_shapes=[pltpu.VMEM((tm, d), BF16)],
        compiler_params=pltpu.CompilerParams(
            dimension_semantics=("parallel", "arbitrary"),
            vmem_limit_bytes=_vmem_limit(pipelined, resident)),
        name="mlp",
    )(x, g.reshape(1, d), w1, w2)


def _upper_tri_ones():
    r = lax.broadcasted_iota(jnp.int32, (LANES, LANES), 0)
    c = lax.broadcasted_iota(jnp.int32, (LANES, LANES), 1)
    return jnp.where(r <= c, 1.0, 0.0).astype(BF16)


def _cumsum_lanes(x, tri):
    hi = x.astype(BF16)
    r1 = x - hi.astype(F32)
    mid = r1.astype(BF16)
    lo = (r1 - mid.astype(F32)).astype(BF16)
    out = jnp.dot(hi, tri, preferred_element_type=F32)
    out += jnp.dot(mid, tri, preferred_element_type=F32)
    out += jnp.dot(lo, tri, preferred_element_type=F32)
    return out


def _fox_gate_kernel(f_ref, b_ref, o_ref, carry_scr):
    @pl.when(pl.program_id(1) == 0)
    def _():
        carry_scr[...] = jnp.zeros_like(carry_scr)

    tri = _upper_tri_ones()
    carry = carry_scr[:, 0:1]
    for c in range(f_ref.shape[2] // LANES):
        sl = slice(c * LANES, (c + 1) * LANES)
        ls = _log_sigmoid(f_ref[0, :, sl] + b_ref[...])
        cum = _cumsum_lanes(ls, tri) + carry
        carry = cum[:, LANES - 1:LANES]
        c2 = cum * LOG2E
        hi = c2.astype(BF16).astype(F32)
        mid = (c2 - hi).astype(BF16).astype(F32)
        o_ref[0, 0, :, sl] = hi
        o_ref[1, 0, :, sl] = mid
        o_ref[2, 0, :, sl] = ((c2 - hi) - mid).astype(BF16).astype(F32)
    carry_scr[...] = jnp.broadcast_to(carry, carry_scr.shape)


def _fox_gates(f_rows, b_f):
    bsz, h, s = f_rows.shape
    lb = min(2048, s)
    return pl.pallas_call(
        _fox_gate_kernel,
        grid=(bsz, s // lb),
        in_specs=[pl.BlockSpec((1, h, lb), lambda b, i: (b, 0, i)),
                  pl.BlockSpec((h, 1), lambda b, i: (0, 0))],
        out_specs=pl.BlockSpec((3, 1, h, lb), lambda b, i: (0, b, 0, i)),
        out_shape=jax.ShapeDtypeStruct((3, bsz, h, s), F32),
        scratch_shapes=[pltpu.VMEM((h, LANES), F32)],
        compiler_params=pltpu.CompilerParams(dimension_semantics=("parallel", "arbitrary")),
        name="fox_gates",
    )(f_rows, b_f.reshape(h, 1))


def _mlstm_gate_kernel(i_ref, f_ref, bi_ref, bf_ref, io_ref, bo_ref):
    tri = _upper_tri_ones()
    io_ref[0] = i_ref[0] + bi_ref[...]
    for c in range(f_ref.shape[2] // LANES):
        sl = slice(c * LANES, (c + 1) * LANES)
        bo_ref[0, :, sl] = _cumsum_lanes(_log_sigmoid(f_ref[0, :, sl] + bf_ref[...]), tri)


def _mlstm_gates(i_rows, f_rows, b_i, b_f):
    assert MLSTM_CHUNK == LANES
    bsz, h, s = f_rows.shape
    lb = min(2048, s)
    row = pl.BlockSpec((1, h, lb), lambda b, i: (b, 0, i))
    bias = pl.BlockSpec((h, 1), lambda b, i: (0, 0))
    return pl.pallas_call(
        _mlstm_gate_kernel,
        grid=(bsz, s // lb),
        in_specs=[row, row, bias, bias],
        out_specs=[row, row],
        out_shape=[jax.ShapeDtypeStruct((bsz, h, s), F32)] * 2,
        compiler_params=pltpu.CompilerParams(dimension_semantics=("parallel", "parallel")),
        name="mlstm_gates",
    )(i_rows, f_rows, b_i.reshape(h, 1), b_f.reshape(h, 1))


FOX_CHAINS = 2
FOX_BIAS_LANES = LANES // N_HEADS


def _fox_kernel(q_ref, qx_ref, k_ref, kx_ref, vt_ref, o_ref, qt_scr, st_scr, mx_scr, *, blk):
    qi = pl.program_id(2)
    lane = lax.broadcasted_iota(jnp.int32, qx_ref.shape, 1)
    owner = lax.shift_right_logical(lane, FOX_BIAS_LANES.bit_length() - 1)
    qx = jnp.where(owner == pl.program_id(1), qx_ref[...].astype(F32), 0.0)
    q_aug = jnp.concatenate([q_ref[...].astype(F32), qx], axis=1)
    qt_scr[...] = q_aug.T.astype(BF16)
    chains = range(FOX_CHAINS)
    ones_rows = jnp.ones((BF16_SUBLANES, blk), BF16)

    def col_max(st):
        part = jnp.max(st.reshape(8, blk // 8, blk), axis=0)
        return jnp.max(part, axis=0, keepdims=True)

    def scores_into(slot, j, which):
        off = pl.multiple_of(j * blk, blk)
        k_aug = jnp.concatenate([k_ref[pl.ds(off, blk), :], kx_ref[pl.ds(off, blk), :]], axis=1)
        for u in which:
            st = jnp.dot(k_aug, qt_scr[:, u * blk:(u + 1) * blk], preferred_element_type=F32)
            st_scr[slot, u] = st
            mx_scr[slot, u] = col_max(st)

    def consume(slot, j, carry, which, diagonal_chain=None):
        vt = jnp.concatenate([vt_ref[0, 0, j], ones_rows], axis=0)
        carry = list(carry)
        for u in which:
            m, acc = carry[u]
            st = st_scr[slot, u]
            if u == diagonal_chain:
                key = lax.broadcasted_iota(jnp.int32, st.shape, 0)
                qry = lax.broadcasted_iota(jnp.int32, st.shape, 1)
                st = jnp.where(key <= qry, st, -1e30)
                mx = col_max(st)
            else:
                mx = mx_scr[slot, u]
            m_new = jnp.maximum(m, mx)
            p = jnp.exp2(st - m_new).astype(BF16)
            acc = jnp.exp2(m - m_new) * acc + jnp.dot(vt, p, preferred_element_type=F32)
            carry[u] = (m_new, acc)
        return tuple(carry)

    def pair(i, carry):
        j0 = 2 * i
        scores_into(1, j0 + 1, chains)
        carry = consume(0, j0, carry, chains)
        scores_into(0, j0 + 2, chains)
        return consume(1, j0 + 1, carry, chains)

    init = tuple((jnp.full((1, blk), -1e30, F32), jnp.zeros((HEAD_DIM + BF16_SUBLANES, blk), F32))
                 for _ in chains)
    scores_into(0, 0, chains)
    carry = lax.fori_loop(0, qi, pair, init)
    scores_into(1, 2 * qi + 1, [1])
    carry = consume(0, 2 * qi, carry, chains, diagonal_chain=0)
    carry = consume(1, 2 * qi + 1, carry, [1], diagonal_chain=1)
    for u in chains:
        acc = carry[u][1]
        out = acc[:HEAD_DIM, :] / acc[HEAD_DIM:HEAD_DIM + 1, :]
        o_ref[u * blk:(u + 1) * blk, :] = out.T.astype(o_ref.dtype)


def _fox_bias_lanes(pieces, bsz, s):
    p = pieces.astype(BF16).transpose(1, 3, 2, 0)
    ones = jnp.ones_like(p)
    pad = jnp.zeros((bsz, s, N_HEADS, FOX_BIAS_LANES - 6), BF16)
    qx = jnp.concatenate([p, ones, pad], axis=-1).reshape(bsz * s, LANES)
    kx = jnp.concatenate([ones, -p, pad], axis=-1).reshape(bsz * s, LANES)
    return qx, kx


def _fox_attention(proj, qx, kx, bsz, s):
    t = proj.shape[0]
    blk = 512
    tq = FOX_CHAINS * blk
    assert s % tq == 0
    nb, nq = s // blk, s // tq
    fox_w = N_HEADS * HEAD_DIM
    vt = proj[:, 2 * fox_w:3 * fox_w].reshape(bsz, nb, blk, N_HEADS, HEAD_DIM).transpose(0, 3, 1, 4, 2)
    pipelined = 3 * _nbytes((s, HEAD_DIM), BF16) + 3 * _nbytes((tq, HEAD_DIM), BF16)
    resident = (2 + 4) * FOX_CHAINS * _nbytes((blk, blk), F32)
    return pl.pallas_call(
        functools.partial(_fox_kernel, blk=blk),
        grid=(bsz, N_HEADS, nq),
        in_specs=[
            pl.BlockSpec((tq, HEAD_DIM), lambda b, h, i: (b * nq + i, h)),
            pl.BlockSpec((tq, LANES), lambda b, h, i: (b * nq + i, 0)),
            pl.BlockSpec((s, HEAD_DIM), lambda b, h, i: (b, N_HEADS + h)),
            pl.BlockSpec((s, LANES), lambda b, h, i: (b, 0)),
            pl.BlockSpec((1, 1, nb, HEAD_DIM, blk), lambda b, h, i: (b, h, 0, 0, 0)),
        ],
        out_specs=pl.BlockSpec((tq, HEAD_DIM), lambda b, h, i: (b * nq + i, h)),
        out_shape=jax.ShapeDtypeStruct((t, fox_w), BF16),
        scratch_shapes=[pltpu.VMEM((2 * HEAD_DIM, tq), BF16),
                        pltpu.VMEM((2, FOX_CHAINS, blk, blk), F32),
                        pltpu.VMEM((2, FOX_CHAINS, 1, blk), F32)],
        compiler_params=pltpu.CompilerParams(
            dimension_semantics=("parallel", "parallel", "arbitrary"),
            vmem_limit_bytes=_vmem_limit(pipelined, resident)),
        name="fox_attention",
    )(proj, qx, proj, kx, vt)


def _retention_tables(chunk):
    h = np.arange(N_HEADS, dtype=np.float64)
    log_g = np.log1p(-np.exp2(-5.0 - h))
    pos = np.arange(chunk, dtype=np.float64)
    rel = pos[:, None] - pos[None, :]
    intra = np.where(rel >= 0, np.exp(log_g[:, None, None] * np.maximum(rel, 0.0)), 0.0)
    q_dec = np.exp(log_g[:, None] * (pos + 1.0))[..., None] * np.ones((1, 1, LANES))
    k_dec = np.exp(log_g[:, None] * (chunk - 1.0 - pos))[..., None] * np.ones((1, 1, LANES))
    c_dec = np.exp(log_g * chunk)[:, None, None] * np.ones((1, 1, LANES))
    return tuple(jnp.asarray(a, F32) for a in (intra, q_dec, k_dec, c_dec))


def _rope(t, cos, sin):
    return t * cos + pltpu.roll(t, HEAD_DIM // 2, 1) * sin


def _retention_kernel(q_ref, k_ref, v_ref, g_ref, cos_ref, sin_ref, intra_ref, qd_ref, kd_ref, cd_ref,
                      gr_ref, o_ref, state_scr, *, chunk):
    @pl.when(pl.program_id(2) == 0)
    def _():
        state_scr[...] = jnp.zeros_like(state_scr)

    intra, q_dec, k_dec, c_dec = intra_ref[0], qd_ref[0], kd_ref[0], cd_ref[0]
    state = state_scr[...]
    for c in range(q_ref.shape[0] // chunk):
        rows = slice(c * chunk, (c + 1) * chunk)
        cos, sin = cos_ref[rows, :], sin_ref[rows, :]
        q = _rope(q_ref[rows, :].astype(F32), cos, sin)
        k = _rope(k_ref[rows, :].astype(F32), cos, sin) * (HEAD_DIM ** -0.5)
        v = v_ref[rows, :]
        sc = lax.dot_general(q.astype(BF16), k.astype(BF16), (((1,), (1,)), ((), ())),
                             preferred_element_type=F32) * intra
        o = jnp.dot(sc.astype(BF16), v, preferred_element_type=F32)
        o += jnp.dot((q * q_dec).astype(BF16), state.astype(BF16), preferred_element_type=F32)
        state = c_dec * state + lax.dot_general((k * k_dec).astype(BF16), v, (((0,), (0,)), ((), ())),
                                                preferred_element_type=F32)
        y = o - jnp.mean(o, axis=-1, keepdims=True)
        y = y * lax.rsqrt(jnp.mean(y * y, axis=-1, keepdims=True) + EPS)
        gate = g_ref[rows, :].astype(F32)
        o_ref[rows, :] = (y * gr_ref[...] * (gate * _sigmoid(gate))).astype(o_ref.dtype)
    state_scr[...] = state


def _retention(proj, col0, g_ret, cos_t, sin_t, bsz, s):
    t = proj.shape[0]
    chunk = min(RET_CHUNK, s)
    lb = min(1024, s)
    nb = s // lb
    intra, q_dec, k_dec, c_dec = _retention_tables(chunk)

    def col(group):
        return pl.BlockSpec((lb, HEAD_DIM), lambda b, h, i: (b * nb + i, col0 + group * N_HEADS + h))

    def table(shape):
        return pl.BlockSpec((1,) + shape, lambda b, h, i: (h, 0, 0))

    pipelined = 5 * _nbytes((lb, HEAD_DIM), BF16) + 2 * _nbytes((lb, HEAD_DIM), F32)
    resident = 2 * (_nbytes((chunk, chunk), F32) + 2 * _nbytes((chunk, LANES), F32)) + 8 * _nbytes((chunk, chunk), F32)
    return pl.pallas_call(
        functools.partial(_retention_kernel, chunk=chunk),
        grid=(bsz, N_HEADS, nb),
        in_specs=[
            col(0), col(1), col(2), col(3),
            pl.BlockSpec((lb, HEAD_DIM), lambda b, h, i: (i, 0)),
            pl.BlockSpec((lb, HEAD_DIM), lambda b, h, i: (i, 0)),
            table((chunk, chunk)), table((chunk, LANES)), table((chunk, LANES)), table((1, LANES)),
            pl.BlockSpec((1, HEAD_DIM), lambda b, h, i: (0, h)),
        ],
        out_specs=pl.BlockSpec((lb, HEAD_DIM), lambda b, h, i: (b * nb + i, h)),
        out_shape=jax.ShapeDtypeStruct((t, N_HEADS * HEAD_DIM), BF16),
        scratch_shapes=[pltpu.VMEM((HEAD_DIM, HEAD_DIM), F32)],
        compiler_params=pltpu.CompilerParams(
            dimension_semantics=("parallel", "parallel", "arbitrary"),
            vmem_limit_bytes=_vmem_limit(pipelined, resident)),
        name="retention",
    )(proj, proj, proj, proj, cos_t, sin_t, intra, q_dec, k_dec, c_dec, g_ret.reshape(1, -1))


def _conv_silu(cur, tail, w):
    row8 = lax.broadcasted_iota(jnp.int32, (8, cur.shape[1]), 0)
    out = cur * w[CONV_W - 1:CONV_W, :]
    for shift in range(1, CONV_W):
        rolled = pltpu.roll(cur, shift, 0)
        head = jnp.where(row8 < shift, pltpu.roll(tail, shift, 0), rolled[:8, :])
        shifted = jnp.concatenate([head, rolled[8:, :]], axis=0)
        out += shifted * w[CONV_W - 1 - shift:CONV_W - shift, :]
    return out * _sigmoid(out)


MLSTM_HEADS_PER_STEP = 1


def _mlstm_kernel(q_ref, k_ref, v_ref, og_ref, wq_ref, wk_ref, i_ref, b_ref, gh_ref, o_ref,
                  c_scr, n_scr, m_scr, qt_scr, kt_scr, *, chunk):
    @pl.when(pl.program_id(2) == 0)
    def _():
        c_scr[...] = jnp.zeros_like(c_scr)
        n_scr[...] = jnp.zeros_like(n_scr)
        m_scr[...] = jnp.zeros_like(m_scr)
        qt_scr[...] = jnp.zeros_like(qt_scr)
        kt_scr[...] = jnp.zeros_like(kt_scr)

    row = lax.broadcasted_iota(jnp.int32, (chunk, chunk), 0)
    col = lax.broadcasted_iota(jnp.int32, (chunk, chunk), 1)
    eye = row == col
    causal = col <= row

    def to_col(r):
        return jnp.sum(jnp.where(eye, r, 0.0), axis=1, keepdims=True)

    heads = range(MLSTM_HEADS_PER_STEP)
    c_st = [c_scr[h] for h in heads]
    n_st = [n_scr[h, 0:1, :] for h in heads]
    m_st = [m_scr[h, 0:1, 0:1] for h in heads]
    q_tail = [qt_scr[h] for h in heads]
    k_tail = [kt_scr[h] for h in heads]
    for c in range(q_ref.shape[0] // chunk):
        rows = slice(c * chunk, (c + 1) * chunk)
        for h in heads:
            qk_cols = slice(h * MLSTM_QK, (h + 1) * MLSTM_QK)
            v_cols = slice(h * MLSTM_V, (h + 1) * MLSTM_V)
            q_pre = q_ref[rows, qk_cols].astype(F32)
            k_pre = k_ref[rows, qk_cols].astype(F32)
            q = _conv_silu(q_pre, q_tail[h], wq_ref[:, qk_cols])
            k = _conv_silu(k_pre, k_tail[h], wk_ref[:, qk_cols]) * (MLSTM_QK ** -0.5)
            q_tail[h], k_tail[h] = q_pre[chunk - 8:, :], k_pre[chunk - 8:, :]
            v = v_ref[rows, v_cols]
            i_row = i_ref[h, :, rows]
            b_row = b_ref[h, :, rows]
            b_col = to_col(b_row)
            a_row = i_row - b_row
            log_d = jnp.where(causal, b_col + a_row, -jnp.inf)
            m_t = jnp.maximum(jnp.max(log_d, axis=1, keepdims=True), b_col + m_st[h])
            w_intra = jnp.exp(log_d - m_t)
            w_inter = jnp.exp(b_col + m_st[h] - m_t)
            qb = q.astype(BF16)
            sc = lax.dot_general(qb, k.astype(BF16), (((1,), (1,)), ((), ())),
                                 preferred_element_type=F32) * w_intra
            num = jnp.dot(sc.astype(BF16), v, preferred_element_type=F32)
            num += w_inter * jnp.dot(qb, c_st[h].astype(BF16), preferred_element_type=F32)
            den = jnp.sum(sc, axis=1, keepdims=True) + w_inter * jnp.sum(q * n_st[h], axis=1, keepdims=True)
            hid = num / jnp.maximum(jnp.abs(den), jnp.exp(-m_t))
            b_last = b_row[:, chunk - 1:chunk]
            log_w = b_last + a_row
            m_new = jnp.maximum(b_last + m_st[h], jnp.max(log_w, axis=1, keepdims=True))
            kw = k * to_col(jnp.exp(log_w - m_new))
            dec = jnp.exp(b_last + m_st[h] - m_new)
            c_st[h] = dec * c_st[h] + lax.dot_general(kw.astype(BF16), v, (((0,), (0,)), ((), ())),
                                                      preferred_element_type=F32)
            n_st[h] = dec * n_st[h] + jnp.sum(kw, axis=0, keepdims=True)
            m_st[h] = m_new
            hid = hid * lax.rsqrt(jnp.mean(hid * hid, axis=-1, keepdims=True) + EPS)
            gate = _sigmoid(og_ref[rows, v_cols].astype(F32))
            o_ref[rows, v_cols] = (hid * gh_ref[:, v_cols] * gate).astype(o_ref.dtype)
    for h in heads:
        c_scr[h] = c_st[h]
        n_scr[h] = jnp.broadcast_to(n_st[h], n_scr.shape[1:])
        m_scr[h] = jnp.broadcast_to(m_st[h], m_scr.shape[1:])
        qt_scr[h] = q_tail[h]
        kt_scr[h] = k_tail[h]


def _mlstm(proj, conv_w, i_rows, bcum_rows, g_h, bsz, s):
    t = proj.shape[0]
    chunk = MLSTM_CHUNK
    hp = MLSTM_HEADS_PER_STEP
    n_hp = N_HEADS // hp
    lb = min(512, s)
    nb = s // lb
    qk_w, v_w = hp * MLSTM_QK, hp * MLSTM_V
    vcol = 2 * N_HEADS * MLSTM_QK // v_w
    gates = pl.BlockSpec((hp, 1, lb), lambda b, h, i: (b * n_hp + h, 0, i))
    pipelined = 2 * _nbytes((lb, qk_w), BF16) + 3 * _nbytes((lb, v_w), BF16)
    resident = hp * (2 * _nbytes((MLSTM_QK, MLSTM_V), F32) + 10 * _nbytes((chunk, MLSTM_V), F32))
    return pl.pallas_call(
        functools.partial(_mlstm_kernel, chunk=chunk),
        grid=(bsz, n_hp, nb),
        in_specs=[
            pl.BlockSpec((lb, qk_w), lambda b, h, i: (b * nb + i, h)),
            pl.BlockSpec((lb, qk_w), lambda b, h, i: (b * nb + i, n_hp + h)),
            pl.BlockSpec((lb, v_w), lambda b, h, i: (b * nb + i, vcol + h)),
            pl.BlockSpec((lb, v_w), lambda b, h, i: (b * nb + i, vcol + n_hp + h)),
            pl.BlockSpec((CONV_W, qk_w), lambda b, h, i: (0, h)),
            pl.BlockSpec((CONV_W, qk_w), lambda b, h, i: (0, n_hp + h)),
            gates, gates,
            pl.BlockSpec((1, v_w), lambda b, h, i: (0, h)),
        ],
        out_specs=pl.BlockSpec((lb, v_w), lambda b, h, i: (b * nb + i, h)),
        out_shape=jax.ShapeDtypeStruct((t, N_HEADS * MLSTM_V), BF16),
        scratch_shapes=[
            pltpu.VMEM((hp, MLSTM_QK, MLSTM_V), F32),
            pltpu.VMEM((hp, 8, MLSTM_QK), F32),
            pltpu.VMEM((hp, 8, LANES), F32),
            pltpu.VMEM((hp, 8, MLSTM_QK), F32),
            pltpu.VMEM((hp, 8, MLSTM_QK), F32),
        ],
        compiler_params=pltpu.CompilerParams(
            dimension_semantics=("parallel", "parallel", "arbitrary"),
            vmem_limit_bytes=_vmem_limit(pipelined, resident)),
        name="mlstm",
    )(proj, proj, proj, proj, conv_w, conv_w,
      i_rows.reshape(bsz * N_HEADS, 1, s), bcum_rows.reshape(bsz * N_HEADS, 1, s), g_h.reshape(1, -1))


def _xattn_kernel(q_ref, k_ref, v_ref, o_ref):
    dh = q_ref.shape[1] // N_XATTN
    for h in range(N_XATTN):
        sl = slice(h * dh, (h + 1) * dh)
        s = lax.dot_general(q_ref[:, sl], k_ref[:, sl], (((1,), (1,)), ((), ())), preferred_element_type=F32)
        p = jnp.exp(s - jnp.max(s, axis=-1, keepdims=True))
        p = p / jnp.sum(p, axis=-1, keepdims=True)
        o_ref[:, sl] = jnp.dot(p.astype(BF16), v_ref[:, sl], preferred_element_type=F32).astype(o_ref.dtype)


def _xattn(q, kv, s, n_mem):
    t, d = q.shape
    tm = min(512, s)
    per_seq = s // tm
    pipelined = 2 * _nbytes((tm, d), BF16) + 2 * _nbytes((n_mem, d), BF16)
    resident = 4 * _nbytes((tm, n_mem), F32) + _nbytes((tm, d), F32)
    return pl.pallas_call(
        _xattn_kernel,
        grid=(t // tm,),
        in_specs=[
            pl.BlockSpec((tm, d), lambda i: (i, 0)),
            pl.BlockSpec((n_mem, d), lambda i: (i // per_seq, 0)),
            pl.BlockSpec((n_mem, d), lambda i: (i // per_seq, 1)),
        ],
        out_specs=pl.BlockSpec((tm, d), lambda i: (i, 0)),
        out_shape=jax.ShapeDtypeStruct((t, d), BF16),
        compiler_params=pltpu.CompilerParams(
            dimension_semantics=("parallel",),
            vmem_limit_bytes=_vmem_limit(pipelined, resident)),
        name="xattn",
    )(q, kv, kv)


def _gate_weight(w_cols):
    return jnp.pad(w_cols, ((0, 0), (0, LANES - w_cols.shape[1]))).astype(BF16)


def _rows(gate_cols, bsz, s):
    return gate_cols.reshape(bsz, s, -1).transpose(0, 2, 1)


def _rope_tables(s):
    inv = 1.0 / (ROPE_BASE ** (jnp.arange(0, HEAD_DIM, 2, dtype=F32) / HEAD_DIM))
    ang = jnp.arange(s, dtype=F32)[:, None] * inv[None, :]
    cos, sin = jnp.cos(ang), jnp.sin(ang)
    return jnp.concatenate([cos, cos], axis=-1), jnp.concatenate([-sin, sin], axis=-1)


def _even_mixer(x, g, w_in, b_f, g_q, g_k, g_ret, w_out, rope, bsz, s):
    fox_w = N_HEADS * HEAD_DIM
    w_main = jnp.concatenate([w_in[:, :3 * fox_w], w_in[:, 3 * fox_w + N_HEADS:]], axis=1).astype(BF16)
    w_gate = _gate_weight(w_in[:, 3 * fox_w:3 * fox_w + N_HEADS])
    qk_gain = jnp.concatenate([jnp.tile(g_q * (HEAD_DIM ** -0.5 * LOG2E), N_HEADS), jnp.tile(g_k, N_HEADS)])
    proj, gate = _norm_matmul(x, g, w_main, w_gate=w_gate, group=HEAD_DIM, group_gain=qk_gain,
                              n_norm_cols=2 * fox_w)
    qx, kx = _fox_bias_lanes(_fox_gates(_rows(gate[:, :N_HEADS], bsz, s), b_f), bsz, s)
    ya = _fox_attention(proj, qx, kx, bsz, s)
    yb = _retention(proj, 3 * N_HEADS, g_ret, rope[0], rope[1], bsz, s)
    return _out_proj(ya, 0, yb, 0, w_out.astype(BF16), x)


def _odd_mixer(x, g, w_in, conv_w, b_i, b_f, g_h, w_out, bsz, s):
    n_main = 2 * N_HEADS * MLSTM_QK + 2 * N_HEADS * MLSTM_V
    proj, gate = _norm_matmul(x, g, w_in[:, :n_main].astype(BF16), w_gate=_gate_weight(w_in[:, n_main:]))
    i_rows, bcum_rows = _mlstm_gates(_rows(gate[:, :N_HEADS], bsz, s),
                                     _rows(gate[:, N_HEADS:2 * N_HEADS], bsz, s), b_i, b_f)
    y = _mlstm(proj, conv_w, i_rows, bcum_rows, g_h, bsz, s)
    return _out_proj(y, 0, y, 1, w_out.astype(BF16), x)


def _cross_attention(x, mem, g_x, g_m, wq, wk, wv, wo, g_q, g_k, s):
    d = x.shape[1]
    dh = d // N_XATTN
    n_mem = mem.shape[0] * s // x.shape[0]
    kv = _norm_matmul(mem, g_m, jnp.concatenate([wk, wv], axis=1).astype(BF16),
                      group=dh, group_gain=jnp.tile(g_k, N_XATTN), n_norm_cols=d)
    q = _norm_matmul(x, g_x, wq.astype(BF16), group=dh, group_gain=jnp.tile(g_q * dh ** -0.5, N_XATTN),
                     n_norm_cols=d)
    o = _xattn(q, kv, s, n_mem)
    return _out_proj(o, 0, o, 1, wo.astype(BF16), x)


def kernel(x, mem, norm_mix, norm_xattn, norm_mem, norm_ffn, ev_w_in, ev_b_f, ev_g_q, ev_g_k, ev_g_ret, ev_w_out, od_w_in, od_conv, od_b_i, od_b_f, od_g_h, od_w_out, xa_wq, xa_wk, xa_wv, xa_wo, xa_g_q, xa_g_k, ffn_w1, ffn_w2):
    bsz, s, d = x.shape
    depth = norm_mix.shape[0]
    xf = x.reshape(bsz * s, d)
    memf = mem.reshape(-1, d)
    rope = _rope_tables(s)
    for l in range(depth):
        if l % 2 == 0:
            e = l // 2
            xf = _even_mixer(xf, norm_mix[l], ev_w_in[e], ev_b_f[e], ev_g_q[e], ev_g_k[e], ev_g_ret[e],
                             ev_w_out[e], rope, bsz, s)
        else:
            o = l // 2
            xf = _odd_mixer(xf, norm_mix[l], od_w_in[o], od_conv[o], od_b_i[o], od_b_f[o], od_g_h[o],
                            od_w_out[o], bsz, s)
        xf = _cross_attention(xf, memf, norm_xattn[l], norm_mem[l], xa_wq[l], xa_wk[l], xa_wv[l], xa_wo[l],
                              xa_g_q[l], xa_g_k[l], s)
        xf = _mlp(xf, norm_ffn[l], ffn_w1[l].astype(BF16), ffn_w2[l].astype(BF16))
    return xf.reshape(bsz, s, d)
```

```python
import functools

import numpy as np
import jax
import jax.numpy as jnp
from jax import lax
from jax.experimental import pallas as pl
from jax.experimental.pallas import tpu as pltpu

F32 = jnp.float32
BF16 = jnp.bfloat16

EPS = 1e-6
LOG2E = 1.4426950408889634
LANES = 128
BF16_SUBLANES = 16
HEAD_DIM = 128
N_HEADS = 8
MLSTM_QK = 128
MLSTM_V = 256
N_XATTN = 4
CONV_W = 4
ROPE_BASE = 10000.0
MLSTM_CHUNK = 128
RET_CHUNK = 256
V7X_VMEM_BUDGET = 56 * 1024 * 1024


def _vmem_limit(pipelined_bytes, resident_bytes):
    return int(min(2 * pipelined_bytes + resident_bytes + (4 << 20), V7X_VMEM_BUDGET))


def _nbytes(shape, dtype):
    return int(np.prod(shape)) * jnp.dtype(dtype).itemsize


def _rms_rows(x, g):
    return x * lax.rsqrt(jnp.mean(x * x, axis=-1, keepdims=True) + EPS) * g


def _sigmoid(x):
    return 1.0 / (1.0 + jnp.exp(-x))


def _log_sigmoid(x):
    return jnp.minimum(x, 0.0) - jnp.log1p(jnp.exp(-jnp.abs(x)))


def _norm_matmul_kernel(*refs, group, n_norm_tiles, has_gate):
    x_ref, g_ref, w_ref = refs[:3]
    pos = 3
    wg_ref = eg_ref = gate_ref = None
    if has_gate:
        wg_ref = refs[pos]
        pos += 1
    if group:
        eg_ref = refs[pos]
        pos += 1
    o_ref = refs[pos]
    pos += 1
    if has_gate:
        gate_ref = refs[pos]
        pos += 1
    h_scr = refs[pos]

    j = pl.program_id(1)
    tm, tn = o_ref.shape

    def store(acc, rows, normed):
        if normed:
            for c in range(tn // group):
                sl = slice(c * group, (c + 1) * group)
                o_ref[rows, sl] = _rms_rows(acc[:, sl], eg_ref[:, sl]).astype(o_ref.dtype)
        else:
            o_ref[rows, :] = acc.astype(o_ref.dtype)

    @pl.when(j == 0)
    def _():
        half = tm // 2
        for r in range(tm // half):
            rows = slice(r * half, (r + 1) * half)
            h = _rms_rows(x_ref[rows, :], g_ref[...]).astype(BF16)
            h_scr[rows, :] = h
            if has_gate:
                gate_ref[rows, :] = jnp.dot(h, wg_ref[...], preferred_element_type=F32)
            store(jnp.dot(h, w_ref[...], preferred_element_type=F32), rows, n_norm_tiles > 0)

    @pl.when(j > 0)
    def _():
        acc = jnp.dot(h_scr[...], w_ref[...], preferred_element_type=F32)
        if not group:
            store(acc, slice(None), False)
            return

        @pl.when(j < n_norm_tiles)
        def _():
            store(acc, slice(None), True)

        @pl.when(j >= n_norm_tiles)
        def _():
            store(acc, slice(None), False)


def _norm_matmul(x, g, w, *, w_gate=None, group=0, group_gain=None, n_norm_cols=0, tn=1024):
    t, d = x.shape
    n = w.shape[1]
    tm = min(1024, t)
    tn = min(tn, n)
    has_gate = w_gate is not None
    n_norm_tiles = n_norm_cols // tn if group else 0
    in_specs = [
        pl.BlockSpec((tm, d), lambda i, j: (i, 0)),
        pl.BlockSpec((1, d), lambda i, j: (0, 0)),
        pl.BlockSpec((d, tn), lambda i, j: (0, j)),
    ]
    args = [x, g.reshape(1, d), w]
    if has_gate:
        in_specs.append(pl.BlockSpec((d, LANES), lambda i, j: (0, 0)))
        args.append(w_gate)
    if group:
        last = n_norm_tiles - 1
        in_specs.append(pl.BlockSpec((1, tn), lambda i, j: (0, jnp.minimum(j, last))))
        args.append(group_gain.reshape(1, n_norm_cols))
    out_shape = [jax.ShapeDtypeStruct((t, n), BF16)]
    out_specs = [pl.BlockSpec((tm, tn), lambda i, j: (i, j))]
    if has_gate:
        out_shape.append(jax.ShapeDtypeStruct((t, LANES), F32))
        out_specs.append(pl.BlockSpec((tm, LANES), lambda i, j: (i, 0)))
    pipelined = (_nbytes((tm, d), F32) + _nbytes((d, tn), BF16) + _nbytes((tm, tn), BF16)
                 + _nbytes((tm, LANES), F32) + _nbytes((d, LANES), BF16))
    resident = _nbytes((tm, d), BF16) + 3 * _nbytes((tm, tn), F32) + _nbytes((tm, d), F32)
    outs = pl.pallas_call(
        functools.partial(_norm_matmul_kernel, group=group, n_norm_tiles=n_norm_tiles, has_gate=has_gate),
        grid=(t // tm, n // tn),
        in_specs=in_specs,
        out_specs=out_specs,
        out_shape=out_shape,
        scratch_shapes=[pltpu.VMEM((tm, d), BF16)],
        compiler_params=pltpu.CompilerParams(
            dimension_semantics=("parallel", "arbitrary"),
            vmem_limit_bytes=_vmem_limit(pipelined, resident)),
        name="norm_matmul",
    )(*args)
    return outs if has_gate else outs[0]


def _out_proj_kernel(al_ref, ar_ref, w_ref, x_ref, o_ref):
    kl = al_ref.shape[1]
    acc = jnp.dot(al_ref[...], w_ref[:kl, :], preferred_element_type=F32)
    acc += jnp.dot(ar_ref[...], w_ref[kl:, :], preferred_element_type=F32)
    o_ref[...] = x_ref[...] + acc


def _out_proj(a_left, col_left, a_right, col_right, w, x):
    t, d = x.shape
    k = w.shape[0]
    kh = k // 2
    tm = min(512, t)
    pipelined = 2 * _nbytes((tm, kh), BF16) + 2 * _nbytes((tm, d), F32) + _nbytes((k, d), BF16)
    return pl.pallas_call(
        _out_proj_kernel,
        grid=(t // tm,),
        in_specs=[
            pl.BlockSpec((tm, kh), lambda i: (i, col_left)),
            pl.BlockSpec((tm, kh), lambda i: (i, col_right)),
            pl.BlockSpec((k, d), lambda i: (0, 0)),
            pl.BlockSpec((tm, d), lambda i: (i, 0)),
        ],
        out_specs=pl.BlockSpec((tm, d), lambda i: (i, 0)),
        out_shape=jax.ShapeDtypeStruct((t, d), F32),
        compiler_params=pltpu.CompilerParams(
            dimension_semantics=("parallel",),
            vmem_limit_bytes=_vmem_limit(pipelined, _nbytes((tm, d), F32))),
        name="out_proj",
    )(a_left, a_right, w, x)


def _mlp_kernel(x_ref, g_ref, w1_ref, w2_ref, o_ref, h_scr):
    def hidden_out(h):
        u = jnp.maximum(jnp.dot(h, w1_ref[...], preferred_element_type=F32), 0.0)
        return jnp.dot((u * u).astype(BF16), w2_ref[...], preferred_element_type=F32)

    @pl.when(pl.program_id(1) == 0)
    def _():
        half = x_ref.shape[0] // 2
        for r in range(2):
            rows = slice(r * half, (r + 1) * half)
            x = x_ref[rows, :]
            h = _rms_rows(x, g_ref[...]).astype(BF16)
            h_scr[rows, :] = h
            o_ref[rows, :] = x + hidden_out(h)

    @pl.when(pl.program_id(1) > 0)
    def _():
        o_ref[...] += hidden_out(h_scr[...])


def _mlp(x, g, w1, w2):
    t, d = x.shape
    f = w1.shape[1]
    tm = min(1024, t)
    tf = min(512, f)
    pipelined = 2 * _nbytes((tm, d), F32) + 2 * _nbytes((d, tf), BF16)
    resident = _nbytes((tm, d), BF16) + 2 * _nbytes((tm, tf), F32)
    return pl.pallas_call(
        _mlp_kernel,
        grid=(t // tm, f // tf),
        in_specs=[
            pl.BlockSpec((tm, d), lambda i, j: (i, 0)),
            pl.BlockSpec((1, d), lambda i, j: (0, 0)),
            pl.BlockSpec((d, tf), lambda i, j: (0, j)),
            pl.BlockSpec((tf, d), lambda i, j: (j, 0)),
        ],
        out_specs=pl.BlockSpec((tm, d), lambda i, j: (i, 0)),
        out_shape=jax.ShapeDtypeStruct((t, d), F32),
        scratch_shapes=[pltpu.VMEM((tm, d), BF16)],
        compiler_params=pltpu.CompilerParams(
            dimension_semantics=("parallel", "arbitrary"),
            vmem_limit_bytes=_vmem_limit(pipelined, resident)),
        name="mlp",
    )(x, g.reshape(1, d), w1, w2)


def _upper_tri_ones():
    r = lax.broadcasted_iota(jnp.int32, (LANES, LANES), 0)
    c = lax.broadcasted_iota(jnp.int32, (LANES, LANES), 1)
    return jnp.where(r <= c, 1.0, 0.0).astype(BF16)


def _cumsum_lanes(x, tri):
    hi = x.astype(BF16)
    r1 = x - hi.astype(F32)
    mid = r1.astype(BF16)
    lo = (r1 - mid.astype(F32)).astype(BF16)
    out = jnp.dot(hi, tri, preferred_element_type=F32)
    out += jnp.dot(mid, tri, preferred_element_type=F32)
    out += jnp.dot(lo, tri, preferred_element_type=F32)
    return out


def _fox_gate_kernel(f_ref, b_ref, o_ref, carry_scr):
    @pl.when(pl.program_id(1) == 0)
    def _():
        carry_scr[...] = jnp.zeros_like(carry_scr)

    tri = _upper_tri_ones()
    carry = carry_scr[:, 0:1]
    for c in range(f_ref.shape[2] // LANES):
        sl = slice(c * LANES, (c + 1) * LANES)
        ls = _log_sigmoid(f_ref[0, :, sl] + b_ref[...])
        cum = _cumsum_lanes(ls, tri) + carry
        carry = cum[:, LANES - 1:LANES]
        c2 = cum * LOG2E
        hi = c2.astype(BF16).astype(F32)
        mid = (c2 - hi).astype(BF16).astype(F32)
        o_ref[0, 0, :, sl] = hi
        o_ref[1, 0, :, sl] = mid
        o_ref[2, 0, :, sl] = ((c2 - hi) - mid).astype(BF16).astype(F32)
    carry_scr[...] = jnp.broadcast_to(carry, carry_scr.shape)


def _fox_gates(f_rows, b_f):
    bsz, h, s = f_rows.shape
    lb = min(2048, s)
    return pl.pallas_call(
        _fox_gate_kernel,
        grid=(bsz, s // lb),
        in_specs=[pl.BlockSpec((1, h, lb), lambda b, i: (b, 0, i)),
                  pl.BlockSpec((h, 1), lambda b, i: (0, 0))],
        out_specs=pl.BlockSpec((3, 1, h, lb), lambda b, i: (0, b, 0, i)),
        out_shape=jax.ShapeDtypeStruct((3, bsz, h, s), F32),
        scratch_shapes=[pltpu.VMEM((h, LANES), F32)],
        compiler_params=pltpu.CompilerParams(dimension_semantics=("parallel", "arbitrary")),
        name="fox_gates",
    )(f_rows, b_f.reshape(h, 1))


def _prefix_max_lanes(x):
    lane = lax.broadcasted_iota(jnp.int32, x.shape, 1)
    shift = 1
    while shift < LANES:
        x = jnp.maximum(x, jnp.where(lane >= shift, pltpu.roll(x, shift, 1), -jnp.inf))
        shift *= 2
    return x


def _mlstm_gate_kernel(i_ref, f_ref, bi_ref, bf_ref, ao_ref, bo_ref, mo_ref):
    tri = _upper_tri_ones()
    for c in range(f_ref.shape[2] // LANES):
        sl = slice(c * LANES, (c + 1) * LANES)
        b = _cumsum_lanes(_log_sigmoid(f_ref[0, :, sl] + bf_ref[...]), tri)
        a = i_ref[0, :, sl] + bi_ref[...] - b
        bo_ref[0, :, sl] = b
        ao_ref[0, :, sl] = a
        mo_ref[0, :, sl] = _prefix_max_lanes(a)


def _mlstm_gates(i_rows, f_rows, b_i, b_f):
    assert MLSTM_CHUNK == LANES
    bsz, h, s = f_rows.shape
    lb = min(2048, s)
    row = pl.BlockSpec((1, h, lb), lambda b, i: (b, 0, i))
    bias = pl.BlockSpec((h, 1), lambda b, i: (0, 0))
    return pl.pallas_call(
        _mlstm_gate_kernel,
        grid=(bsz, s // lb),
        in_specs=[row, row, bias, bias],
        out_specs=[row, row, row],
        out_shape=[jax.ShapeDtypeStruct((bsz, h, s), F32)] * 3,
        compiler_params=pltpu.CompilerParams(dimension_semantics=("parallel", "parallel")),
        name="mlstm_gates",
    )(i_rows, f_rows, b_i.reshape(h, 1), b_f.reshape(h, 1))


FOX_CHAINS = 2
FOX_BIAS_LANES = LANES // N_HEADS


def _fox_kernel(q_ref, qx_ref, k_ref, kx_ref, vt_ref, o_ref, qt_scr, st_scr, mx_scr, *, blk):
    qi = pl.program_id(2)
    lane = lax.broadcasted_iota(jnp.int32, qx_ref.shape, 1)
    owner = lax.shift_right_logical(lane, FOX_BIAS_LANES.bit_length() - 1)
    qx = jnp.where(owner == pl.program_id(1), qx_ref[...].astype(F32), 0.0)
    q_aug = jnp.concatenate([q_ref[...].astype(F32), qx], axis=1)
    qt_scr[...] = q_aug.T.astype(BF16)
    chains = range(FOX_CHAINS)
    ones_rows = jnp.ones((BF16_SUBLANES, blk), BF16)

    def col_max(st):
        part = jnp.max(st.reshape(blk // 16, 16, blk), axis=0)
        return jnp.max(part, axis=0, keepdims=True)

    def scores_into(slot, j, which):
        off = pl.multiple_of(j * blk, blk)
        k_aug = jnp.concatenate([k_ref[pl.ds(off, blk), :], kx_ref[pl.ds(off, blk), :]], axis=1)
        for u in which:
            st = jnp.dot(k_aug, qt_scr[:, u * blk:(u + 1) * blk], preferred_element_type=F32)
            st_scr[slot, u] = st
            mx_scr[slot, u] = col_max(st)

    def consume(slot, j, carry, which, diagonal_chain=None):
        vt = jnp.concatenate([vt_ref[0, 0, j], ones_rows], axis=0)
        carry = list(carry)
        for u in which:
            m, acc = carry[u]
            st = st_scr[slot, u]
            if u == diagonal_chain:
                key = lax.broadcasted_iota(jnp.int32, st.shape, 0)
                qry = lax.broadcasted_iota(jnp.int32, st.shape, 1)
                st = jnp.where(key <= qry, st, -1e30)
                mx = col_max(st)
            else:
                mx = mx_scr[slot, u]
            m_new = jnp.maximum(m, mx)
            p = jnp.exp2(st - m_new).astype(BF16)
            acc = jnp.exp2(m - m_new) * acc + jnp.dot(vt, p, preferred_element_type=F32)
            carry[u] = (m_new, acc)
        return tuple(carry)

    def pair(i, carry):
        j0 = 2 * i
        scores_into(1, j0 + 1, chains)
        carry = consume(0, j0, carry, chains)
        scores_into(0, j0 + 2, chains)
        return consume(1, j0 + 1, carry, chains)

    init = tuple((jnp.full((1, blk), -1e30, F32), jnp.zeros((HEAD_DIM + BF16_SUBLANES, blk), F32))
                 for _ in chains)
    scores_into(0, 0, chains)
    carry = lax.fori_loop(0, qi, pair, init)
    scores_into(1, 2 * qi + 1, [1])
    carry = consume(0, 2 * qi, carry, chains, diagonal_chain=0)
    carry = consume(1, 2 * qi + 1, carry, [1], diagonal_chain=1)
    for u in chains:
        acc = carry[u][1]
        out = acc[:HEAD_DIM, :] / acc[HEAD_DIM:HEAD_DIM + 1, :]
        o_ref[u * blk:(u + 1) * blk, :] = out.T.astype(o_ref.dtype)


def _fox_bias_lanes(pieces, bsz, s):
    p = pieces.astype(BF16).transpose(1, 3, 2, 0)
    ones = jnp.ones_like(p)
    pad = jnp.zeros((bsz, s, N_HEADS, FOX_BIAS_LANES - 6), BF16)
    qx = jnp.concatenate([p, ones, pad], axis=-1).reshape(bsz * s, LANES)
    kx = jnp.concatenate([ones, -p, pad], axis=-1).reshape(bsz * s, LANES)
    return qx, kx


def _fox_attention(proj, qx, kx, bsz, s):
    t = proj.shape[0]
    blk = 512
    tq = FOX_CHAINS * blk
    assert s % tq == 0
    nb, nq = s // blk, s // tq
    fox_w = N_HEADS * HEAD_DIM
    vt = proj[:, 2 * fox_w:3 * fox_w].reshape(bsz, nb, blk, N_HEADS, HEAD_DIM).transpose(0, 3, 1, 4, 2)
    pipelined = 3 * _nbytes((s, HEAD_DIM), BF16) + 3 * _nbytes((tq, HEAD_DIM), BF16)
    resident = (2 + 4) * FOX_CHAINS * _nbytes((blk, blk), F32)
    return pl.pallas_call(
        functools.partial(_fox_kernel, blk=blk),
        grid=(bsz, N_HEADS, nq),
        in_specs=[
            pl.BlockSpec((tq, HEAD_DIM), lambda b, h, i: (b * nq + i, h)),
            pl.BlockSpec((tq, LANES), lambda b, h, i: (b * nq + i, 0)),
            pl.BlockSpec((s, HEAD_DIM), lambda b, h, i: (b, N_HEADS + h)),
            pl.BlockSpec((s, LANES), lambda b, h, i: (b, 0)),
            pl.BlockSpec((1, 1, nb, HEAD_DIM, blk), lambda b, h, i: (b, h, 0, 0, 0)),
        ],
        out_specs=pl.BlockSpec((tq, HEAD_DIM), lambda b, h, i: (b * nq + i, h)),
        out_shape=jax.ShapeDtypeStruct((t, fox_w), BF16),
        scratch_shapes=[pltpu.VMEM((2 * HEAD_DIM, tq), BF16),
                        pltpu.VMEM((2, FOX_CHAINS, blk, blk), F32),
                        pltpu.VMEM((2, FOX_CHAINS, 1, blk), F32)],
        compiler_params=pltpu.CompilerParams(
            dimension_semantics=("parallel", "parallel", "arbitrary"),
            vmem_limit_bytes=_vmem_limit(pipelined, resident)),
        name="fox_attention",
    )(proj, qx, proj, kx, vt)


def _retention_tables(chunk):
    h = np.arange(N_HEADS, dtype=np.float64)
    log_g = np.log1p(-np.exp2(-5.0 - h))
    pos = np.arange(chunk, dtype=np.float64)
    rel = pos[:, None] - pos[None, :]
    intra = np.where(rel >= 0, np.exp(log_g[:, None, None] * np.maximum(rel, 0.0)), 0.0)
    q_dec = np.exp(log_g[:, None] * (pos + 1.0))[..., None] * np.ones((1, 1, LANES))
    k_dec = np.exp(log_g[:, None] * (chunk - 1.0 - pos))[..., None] * np.ones((1, 1, LANES))
    c_dec = np.exp(log_g * chunk)[:, None, None] * np.ones((1, 1, LANES))
    return tuple(jnp.asarray(a, F32) for a in (intra, q_dec, k_dec, c_dec))


def _rope(t, cos, sin):
    return t * cos + pltpu.roll(t, HEAD_DIM // 2, 1) * sin


def _retention_kernel(q_ref, k_ref, v_ref, g_ref, cos_ref, sin_ref, intra_ref, qd_ref, kd_ref, cd_ref,
                      gr_ref, o_ref, state_scr, *, chunk):
    @pl.when(pl.program_id(2) == 0)
    def _():
        state_scr[...] = jnp.zeros_like(state_scr)

    intra, q_dec, k_dec, c_dec = intra_ref[0], qd_ref[0], kd_ref[0], cd_ref[0]
    state = state_scr[...]
    for c in range(q_ref.shape[0] // chunk):
        rows = slice(c * chunk, (c + 1) * chunk)
        cos, sin = cos_ref[rows, :], sin_ref[rows, :]
        q = _rope(q_ref[rows, :].astype(F32), cos, sin)
        k = _rope(k_ref[rows, :].astype(F32), cos, sin) * (HEAD_DIM ** -0.5)
        v = v_ref[rows, :]
        sc = lax.dot_general(q.astype(BF16), k.astype(BF16), (((1,), (1,)), ((), ())),
                             preferred_element_type=F32) * intra
        o = jnp.dot(sc.astype(BF16), v, preferred_element_type=F32)
        o += jnp.dot((q * q_dec).astype(BF16), state.astype(BF16), preferred_element_type=F32)
        state = c_dec * state + lax.dot_general((k * k_dec).astype(BF16), v, (((0,), (0,)), ((), ())),
                                                preferred_element_type=F32)
        y = o - jnp.mean(o, axis=-1, keepdims=True)
        y = y * lax.rsqrt(jnp.mean(y * y, axis=-1, keepdims=True) + EPS)
        gate = g_ref[rows, :].astype(F32)
        o_ref[rows, :] = (y * gr_ref[...] * (gate * _sigmoid(gate))).astype(o_ref.dtype)
    state_scr[...] = state


def _retention(proj, col0, g_ret, cos_t, sin_t, bsz, s):
    t = proj.shape[0]
    chunk = min(RET_CHUNK, s)
    lb = min(1024, s)
    nb = s // lb
    intra, q_dec, k_dec, c_dec = _retention_tables(chunk)

    def col(group):
        return pl.BlockSpec((lb, HEAD_DIM), lambda b, h, i: (b * nb + i, col0 + group * N_HEADS + h))

    def table(shape):
        return pl.BlockSpec((1,) + shape, lambda b, h, i: (h, 0, 0))

    pipelined = 5 * _nbytes((lb, HEAD_DIM), BF16) + 2 * _nbytes((lb, HEAD_DIM), F32)
    resident = 2 * (_nbytes((chunk, chunk), F32) + 2 * _nbytes((chunk, LANES), F32)) + 8 * _nbytes((chunk, chunk), F32)
    return pl.pallas_call(
        functools.partial(_retention_kernel, chunk=chunk),
        grid=(bsz, N_HEADS, nb),
        in_specs=[
            col(0), col(1), col(2), col(3),
            pl.BlockSpec((lb, HEAD_DIM), lambda b, h, i: (i, 0)),
            pl.BlockSpec((lb, HEAD_DIM), lambda b, h, i: (i, 0)),
            table((chunk, chunk)), table((chunk, LANES)), table((chunk, LANES)), table((1, LANES)),
            pl.BlockSpec((1, HEAD_DIM), lambda b, h, i: (0, h)),
        ],
        out_specs=pl.BlockSpec((lb, HEAD_DIM), lambda b, h, i: (b * nb + i, h)),
        out_shape=jax.ShapeDtypeStruct((t, N_HEADS * HEAD_DIM), BF16),
        scratch_shapes=[pltpu.VMEM((HEAD_DIM, HEAD_DIM), F32)],
        compiler_params=pltpu.CompilerParams(
            dimension_semantics=("parallel", "parallel", "arbitrary"),
            vmem_limit_bytes=_vmem_limit(pipelined, resident)),
        name="retention",
    )(proj, proj, proj, proj, cos_t, sin_t, intra, q_dec, k_dec, c_dec, g_ret.reshape(1, -1))


def _conv_silu(cur, tail, w):
    row8 = lax.broadcasted_iota(jnp.int32, (8, cur.shape[1]), 0)
    out = cur * w[CONV_W - 1:CONV_W, :]
    for shift in range(1, CONV_W):
        rolled = pltpu.roll(cur, shift, 0)
        head = jnp.where(row8 < shift, pltpu.roll(tail, shift, 0), rolled[:8, :])
        shifted = jnp.concatenate([head, rolled[8:, :]], axis=0)
        out += shifted * w[CONV_W - 1 - shift:CONV_W - shift, :]
    return out * _sigmoid(out)


def _split3(x):
    hi = x.astype(BF16).astype(F32)
    rest = x - hi
    mid = rest.astype(BF16).astype(F32)
    return hi, mid, (rest - mid).astype(BF16).astype(F32)


def _rows16(assign):
    sub = lax.broadcasted_iota(jnp.int32, (BF16_SUBLANES, LANES), 0)
    out = jnp.zeros((BF16_SUBLANES, LANES), F32)
    for r, val in assign.items():
        out = jnp.where(sub == r, val, out)
    return out


def _mlstm_kernel(q_ref, k_ref, v_ref, og_ref, wq_ref, wk_ref, a_ref, b_ref, am_ref, gh_ref, o_ref,
                  c_scr, m_scr, tail_scr, *, chunk):
    @pl.when(pl.program_id(2) == 0)
    def _():
        c_scr[...] = jnp.zeros_like(c_scr)
        m_scr[...] = jnp.zeros_like(m_scr)
        tail_scr[...] = jnp.zeros_like(tail_scr)

    dv = MLSTM_V
    causal = (lax.broadcasted_iota(jnp.int32, (chunk, chunk), 1)
              <= lax.broadcasted_iota(jnp.int32, (chunk, chunk), 0))
    ones_cols = jnp.ones((chunk, LANES), BF16)
    mean_cols = jnp.ones((dv, LANES), BF16)
    rep_g = _rows16({0: 1.0, 1: 1.0, 2: 1.0})
    rep_b = _rows16({3: 1.0, 4: 1.0, 5: 1.0})
    rep_a = _rows16({6: 1.0, 7: 1.0, 8: 1.0})

    chunks = range(q_ref.shape[0] // chunk)
    halves = (slice(0, LANES), slice(LANES, dv))

    def rows(c):
        return slice(c * chunk, (c + 1) * chunk)

    tail = tail_scr[...]
    w_qk = jnp.concatenate([wq_ref[...], wk_ref[...]], axis=1)
    q_b, k_f, sc_raw, v_aug, b_row, a_row, a_cummax, b_last, lw_max = ([] for _ in range(9))
    for c in chunks:
        cur = jnp.concatenate([q_ref[rows(c), :], k_ref[rows(c), :]], axis=1).astype(F32)
        qk = _conv_silu(cur, tail, w_qk)
        tail = cur[chunk - 8:, :]
        q_b.append(qk[:, :MLSTM_QK].astype(BF16))
        k_f.append(qk[:, MLSTM_QK:] * (MLSTM_QK ** -0.5))
        sc_raw.append(lax.dot_general(q_b[c], k_f[c].astype(BF16), (((1,), (1,)), ((), ())),
                                      preferred_element_type=F32))
        v_aug.append(jnp.concatenate([v_ref[rows(c), :], ones_cols], axis=1))
        b_row.append(b_ref[0, :, rows(c)])
        a_row.append(a_ref[0, :, rows(c)])
        a_cummax.append(am_ref[0, :, rows(c)])
        b_last.append(b_row[c][:, chunk - 1:chunk])
        lw_max.append(b_last[c] + a_cummax[c][:, chunk - 1:chunk])
    tail_scr[...] = tail

    m_in, m_out = [], []
    m_st = m_scr[0:1, 0:1]
    for c in chunks:
        m_in.append(m_st)
        m_st = jnp.maximum(b_last[c] + m_st, lw_max[c])
        m_out.append(m_st)
    m_scr[...] = jnp.broadcast_to(m_st, m_scr.shape)

    rep = []
    for c in chunks:
        g3, b3, a3 = _split3(jnp.maximum(a_cummax[c], m_in[c])), _split3(b_row[c]), _split3(a_row[c])
        lhs = _rows16({0: g3[0], 1: g3[1], 2: g3[2], 3: b3[0], 4: b3[1], 5: b3[2],
                       6: a3[0], 7: a3[1], 8: a3[2], 9: 1.0, 10: 1.0, 11: 1.0})
        rep_e = _rows16({0: -1.0, 1: -1.0, 2: -1.0, 9: a3[0], 10: a3[1], 11: a3[2]})
        rhs = jnp.concatenate([rep_e, rep_g, rep_b, rep_a], axis=1)
        rep.append(lax.dot_general(lhs.astype(BF16), rhs.astype(BF16), (((0,), (0,)), ((), ())),
                                   preferred_element_type=F32))

    num, inc = [], []
    for c in chunks:
        e_mat, a_rep = rep[c][:, 0:LANES], rep[c][:, 3 * LANES:]
        sc = jnp.where(causal, sc_raw[c] * jnp.exp(e_mat), 0.0)
        num.append(jnp.dot(sc.astype(BF16), v_aug[c], preferred_element_type=F32))
        kw = k_f[c] * jnp.exp(b_last[c] + a_rep - m_out[c])
        inc.append(lax.dot_general(kw.astype(BF16), v_aug[c], (((0,), (0,)), ((), ())),
                                   preferred_element_type=F32))

    c_in = []
    c_aug = c_scr[...]
    for c in chunks:
        c_in.append(c_aug.astype(BF16))
        c_aug = jnp.exp(b_last[c] + m_in[c] - m_out[c]) * c_aug + inc[c]
    c_scr[...] = c_aug

    hid, sq_sum = [], []
    for c in chunks:
        g_rep, b_rep = rep[c][:, LANES:2 * LANES], rep[c][:, 2 * LANES:3 * LANES]
        w_inter = jnp.exp(m_in[c] - g_rep)
        inter = jnp.dot(q_b[c], c_in[c], preferred_element_type=F32)
        den = num[c][:, dv:] + w_inter * inter[:, dv:]
        inv = 1.0 / jnp.maximum(jnp.abs(den), jnp.exp(-(b_rep + g_rep)))
        hid.append([(num[c][:, sl] + w_inter * inter[:, sl]) * inv for sl in halves])
        sq = jnp.concatenate([hh * hh for hh in hid[c]], axis=1).astype(BF16)
        sq_sum.append(jnp.dot(sq, mean_cols, preferred_element_type=F32))

    for c in chunks:
        scale = lax.rsqrt(sq_sum[c] * (1.0 / dv) + EPS)
        for part, sl in zip(hid[c], halves):
            gate = _sigmoid(og_ref[rows(c), sl].astype(F32))
            o_ref[rows(c), sl] = (part * scale * gh_ref[:, sl] * gate).astype(o_ref.dtype)


def _mlstm(proj, conv_w, gate_rows, g_h, bsz, s):
    t = proj.shape[0]
    chunk = MLSTM_CHUNK
    assert chunk == LANES and MLSTM_QK == LANES
    lb = min(512, s)
    nb = s // lb
    vcol = 2 * N_HEADS * MLSTM_QK // MLSTM_V
    gates = pl.BlockSpec((1, 1, lb), lambda b, h, i: (b * N_HEADS + h, 0, i))
    pipelined = 2 * _nbytes((lb, MLSTM_QK), BF16) + 3 * _nbytes((lb, MLSTM_V), BF16)
    resident = 2 * _nbytes((MLSTM_QK, MLSTM_V + LANES), F32) + 16 * _nbytes((chunk, MLSTM_V + LANES), F32)
    return pl.pallas_call(
        functools.partial(_mlstm_kernel, chunk=chunk),
        grid=(bsz, N_HEADS, nb),
        in_specs=[
            pl.BlockSpec((lb, MLSTM_QK), lambda b, h, i: (b * nb + i, h)),
            pl.BlockSpec((lb, MLSTM_QK), lambda b, h, i: (b * nb + i, N_HEADS + h)),
            pl.BlockSpec((lb, MLSTM_V), lambda b, h, i: (b * nb + i, vcol + h)),
            pl.BlockSpec((lb, MLSTM_V), lambda b, h, i: (b * nb + i, vcol + N_HEADS + h)),
            pl.BlockSpec((CONV_W, MLSTM_QK), lambda b, h, i: (0, h)),
            pl.BlockSpec((CONV_W, MLSTM_QK), lambda b, h, i: (0, N_HEADS + h)),
            gates, gates, gates,
            pl.BlockSpec((1, MLSTM_V), lambda b, h, i: (0, h)),
        ],
        out_specs=pl.BlockSpec((lb, MLSTM_V), lambda b, h, i: (b * nb + i, h)),
        out_shape=jax.ShapeDtypeStruct((t, N_HEADS * MLSTM_V), BF16),
        scratch_shapes=[
            pltpu.VMEM((MLSTM_QK, MLSTM_V + LANES), F32),
            pltpu.VMEM((8, LANES), F32),
            pltpu.VMEM((8, 2 * MLSTM_QK), F32),
        ],
        compiler_params=pltpu.CompilerParams(
            dimension_semantics=("parallel", "parallel", "arbitrary"),
            vmem_limit_bytes=_vmem_limit(pipelined, resident)),
        name="mlstm",
    )(proj, proj, proj, proj, conv_w, conv_w,
      *(r.reshape(bsz * N_HEADS, 1, s) for r in gate_rows), g_h.reshape(1, -1))


def _xattn_kernel(q_ref, k_ref, v_ref, o_ref):
    dh = q_ref.shape[1] // N_XATTN
    for h in range(N_XATTN):
        sl = slice(h * dh, (h + 1) * dh)
        s = lax.dot_general(q_ref[:, sl], k_ref[:, sl], (((1,), (1,)), ((), ())), preferred_element_type=F32)
        p = jnp.exp(s - jnp.max(s, axis=-1, keepdims=True))
        p = p / jnp.sum(p, axis=-1, keepdims=True)
        o_ref[:, sl] = jnp.dot(p.astype(BF16), v_ref[:, sl], preferred_element_type=F32).astype(o_ref.dtype)


def _xattn(q, kv, s, n_mem):
    t, d = q.shape
    tm = min(512, s)
    per_seq = s // tm
    pipelined = 2 * _nbytes((tm, d), BF16) + 2 * _nbytes((n_mem, d), BF16)
    resident = 4 * _nbytes((tm, n_mem), F32) + _nbytes((tm, d), F32)
    return pl.pallas_call(
        _xattn_kernel,
        grid=(t // tm,),
        in_specs=[
            pl.BlockSpec((tm, d), lambda i: (i, 0)),
            pl.BlockSpec((n_mem, d), lambda i: (i // per_seq, 0)),
            pl.BlockSpec((n_mem, d), lambda i: (i // per_seq, 1)),
        ],
        out_specs=pl.BlockSpec((tm, d), lambda i: (i, 0)),
        out_shape=jax.ShapeDtypeStruct((t, d), BF16),
        compiler_params=pltpu.CompilerParams(
            dimension_semantics=("parallel",),
            vmem_limit_bytes=_vmem_limit(pipelined, resident)),
        name="xattn",
    )(q, kv, kv)


def _gate_weight(w_cols):
    return jnp.pad(w_cols, ((0, 0), (0, LANES - w_cols.shape[1]))).astype(BF16)


def _rows(gate_cols, bsz, s):
    return gate_cols.reshape(bsz, s, -1).transpose(0, 2, 1)


def _rope_tables(s):
    inv = 1.0 / (ROPE_BASE ** (jnp.arange(0, HEAD_DIM, 2, dtype=F32) / HEAD_DIM))
    ang = jnp.arange(s, dtype=F32)[:, None] * inv[None, :]
    cos, sin = jnp.cos(ang), jnp.sin(ang)
    return jnp.concatenate([cos, cos], axis=-1), jnp.concatenate([-sin, sin], axis=-1)


def _even_mixer(x, g, w_in, b_f, g_q, g_k, g_ret, w_out, rope, bsz, s):
    fox_w = N_HEADS * HEAD_DIM
    w_main = jnp.concatenate([w_in[:, :3 * fox_w], w_in[:, 3 * fox_w + N_HEADS:]], axis=1).astype(BF16)
    w_gate = _gate_weight(w_in[:, 3 * fox_w:3 * fox_w + N_HEADS])
    qk_gain = jnp.concatenate([jnp.tile(g_q * (HEAD_DIM ** -0.5 * LOG2E), N_HEADS), jnp.tile(g_k, N_HEADS)])
    proj, gate = _norm_matmul(x, g, w_main, w_gate=w_gate, group=HEAD_DIM, group_gain=qk_gain,
                              n_norm_cols=2 * fox_w)
    qx, kx = _fox_bias_lanes(_fox_gates(_rows(gate[:, :N_HEADS], bsz, s), b_f), bsz, s)
    ya = _fox_attention(proj, qx, kx, bsz, s)
    yb = _retention(proj, 3 * N_HEADS, g_ret, rope[0], rope[1], bsz, s)
    return _out_proj(ya, 0, yb, 0, w_out.astype(BF16), x)


def _odd_mixer(x, g, w_in, conv_w, b_i, b_f, g_h, w_out, bsz, s):
    n_main = 2 * N_HEADS * MLSTM_QK + 2 * N_HEADS * MLSTM_V
    proj, gate = _norm_matmul(x, g, w_in[:, :n_main].astype(BF16), w_gate=_gate_weight(w_in[:, n_main:]))
    gate_rows = _mlstm_gates(_rows(gate[:, :N_HEADS], bsz, s),
                             _rows(gate[:, N_HEADS:2 * N_HEADS], bsz, s), b_i, b_f)
    y = _mlstm(proj, conv_w, gate_rows, g_h, bsz, s)
    return _out_proj(y, 0, y, 1, w_out.astype(BF16), x)


def _cross_attention(x, mem, g_x, g_m, wq, wk, wv, wo, g_q, g_k, s):
    d = x.shape[1]
    dh = d // N_XATTN
    n_mem = mem.shape[0] * s // x.shape[0]
    kv = _norm_matmul(mem, g_m, jnp.concatenate([wk, wv], axis=1).astype(BF16),
                      group=dh, group_gain=jnp.tile(g_k, N_XATTN), n_norm_cols=d)
    q = _norm_matmul(x, g_x, wq.astype(BF16), group=dh, group_gain=jnp.tile(g_q * dh ** -0.5, N_XATTN),
                     n_norm_cols=d)
    o = _xattn(q, kv, s, n_mem)
    return _out_proj(o, 0, o, 1, wo.astype(BF16), x)


def kernel(x, mem, norm_mix, norm_xattn, norm_mem, norm_ffn, ev_w_in, ev_b_f, ev_g_q, ev_g_k, ev_g_ret, ev_w_out, od_w_in, od_conv, od_b_i, od_b_f, od_g_h, od_w_out, xa_wq, xa_wk, xa_wv, xa_wo, xa_g_q, xa_g_k, ffn_w1, ffn_w2):
    bsz, s, d = x.shape
    depth = norm_mix.shape[0]
    xf = x.reshape(bsz * s, d)
    memf = mem.reshape(-1, d)
    rope = _rope_tables(s)
    for l in range(depth):
        if l % 2 == 0:
            e = l // 2
            xf = _even_mixer(xf, norm_mix[l], ev_w_in[e], ev_b_f[e], ev_g_q[e], ev_g_k[e], ev_g_ret[e],
                             ev_w_out[e], rope, bsz, s)
        else:
            o = l // 2
            xf = _odd_mixer(xf, norm_mix[l], od_w_in[o], od_conv[o], od_b_i[o], od_b_f[o], od_g_h[o],
                            od_w_out[o], bsz, s)
        xf = _cross_attention(xf, memf, norm_xattn[l], norm_mem[l], xa_wq[l], xa_wk[l], xa_wv[l], xa_wo[l],
                              xa_g_q[l], xa_g_k[l], s)
        xf = _mlp(xf, norm_ffn[l], ffn_w1[l].astype(BF16), ffn_w2[l].astype(BF16))
    return xf.reshape(bsz, s, d)
```

```python
import functools

import numpy as np
import jax
import jax.numpy as jnp
from jax import lax
from jax.experimental import pallas as pl
from jax.experimental.pallas import tpu as pltpu

F32 = jnp.float32
BF16 = jnp.bfloat16

EPS = 1e-6
LOG2E = 1.4426950408889634
LANES = 128
BF16_SUBLANES = 16
HEAD_DIM = 128
N_HEADS = 8
MLSTM_QK = 128
MLSTM_V = 256
N_XATTN = 4
CONV_W = 4
ROPE_BASE = 10000.0
MLSTM_CHUNK = 128
RET_CHUNK = 256
V7X_VMEM_BUDGET = 60 * 1024 * 1024
NORM_ROW_PARTS = 4


def _vmem_limit(pipelined_bytes, resident_bytes):
    return int(min(2 * pipelined_bytes + resident_bytes + (4 << 20), V7X_VMEM_BUDGET))


def _nbytes(shape, dtype):
    return int(np.prod(shape)) * jnp.dtype(dtype).itemsize


def _rms_rows(x, g):
    return x * lax.rsqrt(jnp.mean(x * x, axis=-1, keepdims=True) + EPS) * g


def _sigmoid(x):
    return 1.0 / (1.0 + jnp.exp(-x))


def _log_sigmoid(x):
    return jnp.minimum(x, 0.0) - jnp.log1p(jnp.exp(-jnp.abs(x)))


def _norm_matmul_kernel(*refs, group, n_norm_tiles, has_gate):
    x_ref, g_ref, w_ref = refs[:3]
    pos = 3
    wg_ref = eg_ref = gate_ref = None
    if has_gate:
        wg_ref = refs[pos]
        pos += 1
    if group:
        eg_ref = refs[pos]
        pos += 1
    o_ref = refs[pos]
    pos += 1
    if has_gate:
        gate_ref = refs[pos]
        pos += 1
    h_scr = refs[pos]

    j = pl.program_id(1)
    tm, tn = o_ref.shape

    def store(acc, rows, normed):
        if normed:
            for c in range(tn // group):
                sl = slice(c * group, (c + 1) * group)
                o_ref[rows, sl] = _rms_rows(acc[:, sl], eg_ref[:, sl]).astype(o_ref.dtype)
        else:
            o_ref[rows, :] = acc.astype(o_ref.dtype)

    @pl.when(j == 0)
    def _():
        part = tm // NORM_ROW_PARTS
        for r in range(NORM_ROW_PARTS):
            rows = slice(r * part, (r + 1) * part)
            h = _rms_rows(x_ref[rows, :], g_ref[...]).astype(BF16)
            h_scr[rows, :] = h
            if has_gate:
                gate_ref[rows, :] = jnp.dot(h, wg_ref[...], preferred_element_type=F32)
            store(jnp.dot(h, w_ref[...], preferred_element_type=F32), rows, n_norm_tiles > 0)

    @pl.when(j > 0)
    def _():
        acc = jnp.dot(h_scr[...], w_ref[...], preferred_element_type=F32)
        if not group:
            store(acc, slice(None), False)
            return

        @pl.when(j < n_norm_tiles)
        def _():
            store(acc, slice(None), True)

        @pl.when(j >= n_norm_tiles)
        def _():
            store(acc, slice(None), False)


def _norm_matmul(x, g, w, *, w_gate=None, group=0, group_gain=None, n_norm_cols=0):
    t, d = x.shape
    n = w.shape[1]
    tm = min(1024, t)
    tn = 2048 if (n % 2048 == 0 and n_norm_cols % 2048 == 0) else 1024
    has_gate = w_gate is not None
    n_norm_tiles = n_norm_cols // tn if group else 0
    in_specs = [
        pl.BlockSpec((tm, d), lambda i, j: (i, 0)),
        pl.BlockSpec((1, d), lambda i, j: (0, 0)),
        pl.BlockSpec((d, tn), lambda i, j: (0, j)),
    ]
    args = [x, g.reshape(1, d), w]
    if has_gate:
        in_specs.append(pl.BlockSpec((d, LANES), lambda i, j: (0, 0)))
        args.append(w_gate)
    if group:
        last = n_norm_tiles - 1
        in_specs.append(pl.BlockSpec((1, tn), lambda i, j: (0, jnp.minimum(j, last))))
        args.append(group_gain.reshape(1, n_norm_cols))
    out_shape = [jax.ShapeDtypeStruct((t, n), BF16)]
    out_specs = [pl.BlockSpec((tm, tn), lambda i, j: (i, j))]
    if has_gate:
        out_shape.append(jax.ShapeDtypeStruct((t, LANES), F32))
        out_specs.append(pl.BlockSpec((tm, LANES), lambda i, j: (i, 0)))
    pipelined = (_nbytes((tm, d), F32) + _nbytes((d, tn), BF16) + _nbytes((tm, tn), BF16)
                 + _nbytes((tm, LANES), F32) + _nbytes((d, LANES), BF16))
    resident = _nbytes((tm, d), BF16) + 3 * _nbytes((tm, tn), F32) + _nbytes((tm, d), F32)
    outs = pl.pallas_call(
        functools.partial(_norm_matmul_kernel, group=group, n_norm_tiles=n_norm_tiles, has_gate=has_gate),
        grid=(t // tm, n // tn),
        in_specs=in_specs,
        out_specs=out_specs,
        out_shape=out_shape,
        scratch_shapes=[pltpu.VMEM((tm, d), BF16)],
        compiler_params=pltpu.CompilerParams(
            dimension_semantics=("parallel", "arbitrary"),
            vmem_limit_bytes=_vmem_limit(pipelined, resident)),
        name="norm_matmul",
    )(*args)
    return outs if has_gate else outs[0]


def _out_proj_kernel(al_ref, ar_ref, w_ref, x_ref, o_ref):
    kl = al_ref.shape[1]
    acc = jnp.dot(al_ref[...], w_ref[:kl, :], preferred_element_type=F32)
    acc += jnp.dot(ar_ref[...], w_ref[kl:, :], preferred_element_type=F32)
    o_ref[...] = x_ref[...] + acc


def _out_proj(a_left, col_left, a_right, col_right, w, x):
    t, d = x.shape
    k = w.shape[0]
    kh = k // 2
    tm = min(512, t)
    pipelined = 2 * _nbytes((tm, kh), BF16) + 2 * _nbytes((tm, d), F32) + _nbytes((k, d), BF16)
    return pl.pallas_call(
        _out_proj_kernel,
        grid=(t // tm,),
        in_specs=[
            pl.BlockSpec((tm, kh), lambda i: (i, col_left)),
            pl.BlockSpec((tm, kh), lambda i: (i, col_right)),
            pl.BlockSpec((k, d), lambda i: (0, 0)),
            pl.BlockSpec((tm, d), lambda i: (i, 0)),
        ],
        out_specs=pl.BlockSpec((tm, d), lambda i: (i, 0)),
        out_shape=jax.ShapeDtypeStruct((t, d), F32),
        compiler_params=pltpu.CompilerParams(
            dimension_semantics=("parallel",),
            vmem_limit_bytes=_vmem_limit(pipelined, _nbytes((tm, d), F32))),
        name="out_proj",
    )(a_left, a_right, w, x)


def _mlp_kernel(x_ref, g_ref, w1_ref, w2_ref, o_ref, h_scr):
    def hidden_out(h):
        u = jnp.maximum(jnp.dot(h, w1_ref[...], preferred_element_type=F32), 0.0)
        return jnp.dot((u * u).astype(BF16), w2_ref[...], preferred_element_type=F32)

    @pl.when(pl.program_id(1) == 0)
    def _():
        part = x_ref.shape[0] // NORM_ROW_PARTS
        for r in range(NORM_ROW_PARTS):
            rows = slice(r * part, (r + 1) * part)
            x = x_ref[rows, :]
            h = _rms_rows(x, g_ref[...]).astype(BF16)
            h_scr[rows, :] = h
            o_ref[rows, :] = x + hidden_out(h)

    @pl.when(pl.program_id(1) > 0)
    def _():
        o_ref[...] += hidden_out(h_scr[...])


def _mlp(x, g, w1, w2):
    t, d = x.shape
    f = w1.shape[1]
    tm = min(1024, t)
    tf = min(1024, f)
    pipelined = 2 * _nbytes((tm, d), F32) + 2 * _nbytes((d, tf), BF16)
    resident = _nbytes((tm, d), BF16) + 2 * _nbytes((tm, tf), F32)
    return pl.pallas_call(
        _mlp_kernel,
        grid=(t // tm, f // tf),
        in_specs=[
            pl.BlockSpec((tm, d), lambda i, j: (i, 0)),
            pl.BlockSpec((1, d), lambda i, j: (0, 0)),
            pl.BlockSpec((d, tf), lambda i, j: (0, j)),
            pl.BlockSpec((tf, d), lambda i, j: (j, 0)),
        ],
        out_specs=pl.BlockSpec((tm, d), lambda i, j: (i, 0)),
        out_shape=jax.ShapeDtypeStruct((t, d), F32),
        scratch_shapes=[pltpu.VMEM((tm, d), BF16)],
        compiler_params=pltpu.CompilerParams(
            dimension_semantics=("parallel", "arbitrary"),
            vmem_limit_bytes=_vmem_limit(pipelined, resident)),
        name="mlp",
    )(x, g.reshape(1, d), w1, w2)


def _upper_tri_ones():
    r = lax.broadcasted_iota(jnp.int32, (LANES, LANES), 0)
    c = lax.broadcasted_iota(jnp.int32, (LANES, LANES), 1)
    return jnp.where(r <= c, 1.0, 0.0).astype(BF16)


def _cumsum_lanes(x, tri):
    hi = x.astype(BF16)
    r1 = x - hi.astype(F32)
    mid = r1.astype(BF16)
    lo = (r1 - mid.astype(F32)).astype(BF16)
    out = jnp.dot(hi, tri, preferred_element_type=F32)
    out += jnp.dot(mid, tri, preferred_element_type=F32)
    out += jnp.dot(lo, tri, preferred_element_type=F32)
    return out


def _fox_gate_kernel(f_ref, b_ref, o_ref, carry_scr):
    @pl.when(pl.program_id(1) == 0)
    def _():
        carry_scr[...] = jnp.zeros_like(carry_scr)

    tri = _upper_tri_ones()
    carry = carry_scr[:, 0:1]
    for c in range(f_ref.shape[2] // LANES):
        sl = slice(c * LANES, (c + 1) * LANES)
        ls = _log_sigmoid(f_ref[0, :, sl] + b_ref[...])
        cum = _cumsum_lanes(ls, tri) + carry
        carry = cum[:, LANES - 1:LANES]
        c2 = cum * LOG2E
        hi = c2.astype(BF16).astype(F32)
        mid = (c2 - hi).astype(BF16).astype(F32)
        o_ref[0, 0, :, sl] = hi
        o_ref[1, 0, :, sl] = mid
        o_ref[2, 0, :, sl] = ((c2 - hi) - mid).astype(BF16).astype(F32)
    carry_scr[...] = jnp.broadcast_to(carry, carry_scr.shape)


def _fox_gates(f_rows, b_f):
    bsz, h, s = f_rows.shape
    lb = min(2048, s)
    return pl.pallas_call(
        _fox_gate_kernel,
        grid=(bsz, s // lb),
        in_specs=[pl.BlockSpec((1, h, lb), lambda b, i: (b, 0, i)),
                  pl.BlockSpec((h, 1), lambda b, i: (0, 0))],
        out_specs=pl.BlockSpec((3, 1, h, lb), lambda b, i: (0, b, 0, i)),
        out_shape=jax.ShapeDtypeStruct((3, bsz, h, s), F32),
        scratch_shapes=[pltpu.VMEM((h, LANES), F32)],
        compiler_params=pltpu.CompilerParams(dimension_semantics=("parallel", "arbitrary")),
        name="fox_gates",
    )(f_rows, b_f.reshape(h, 1))


def _prefix_max_lanes(x):
    lane = lax.broadcasted_iota(jnp.int32, x.shape, 1)
    shift = 1
    while shift < LANES:
        x = jnp.maximum(x, jnp.where(lane >= shift, pltpu.roll(x, shift, 1), -jnp.inf))
        shift *= 2
    return x


def _mlstm_gate_kernel(i_ref, f_ref, bi_ref, bf_ref, ao_ref, bo_ref, mo_ref):
    tri = _upper_tri_ones()
    for c in range(f_ref.shape[2] // LANES):
        sl = slice(c * LANES, (c + 1) * LANES)
        b = _cumsum_lanes(_log_sigmoid(f_ref[0, :, sl] + bf_ref[...]), tri)
        a = i_ref[0, :, sl] + bi_ref[...] - b
        bo_ref[0, :, sl] = b
        ao_ref[0, :, sl] = a
        mo_ref[0, :, sl] = _prefix_max_lanes(a)


def _mlstm_gates(i_rows, f_rows, b_i, b_f):
    assert MLSTM_CHUNK == LANES
    bsz, h, s = f_rows.shape
    lb = min(2048, s)
    row = pl.BlockSpec((1, h, lb), lambda b, i: (b, 0, i))
    bias = pl.BlockSpec((h, 1), lambda b, i: (0, 0))
    return pl.pallas_call(
        _mlstm_gate_kernel,
        grid=(bsz, s // lb),
        in_specs=[row, row, bias, bias],
        out_specs=[row, row, row],
        out_shape=[jax.ShapeDtypeStruct((bsz, h, s), F32)] * 3,
        compiler_params=pltpu.CompilerParams(dimension_semantics=("parallel", "parallel")),
        name="mlstm_gates",
    )(i_rows, f_rows, b_i.reshape(h, 1), b_f.reshape(h, 1))


FOX_CHAINS = 2
FOX_BIAS_LANES = LANES // N_HEADS


def _fox_kernel(q_ref, qx_ref, k_ref, kx_ref, vt_ref, o_ref, qt_scr, st_scr, mx_scr, *, blk):
    qi = pl.program_id(2)
    lane = lax.broadcasted_iota(jnp.int32, qx_ref.shape, 1)
    owner = lax.shift_right_logical(lane, FOX_BIAS_LANES.bit_length() - 1)
    qx = jnp.where(owner == pl.program_id(1), qx_ref[...].astype(F32), 0.0)
    q_aug = jnp.concatenate([q_ref[...].astype(F32), qx], axis=1)
    qt_scr[...] = q_aug.T.astype(BF16)
    chains = range(FOX_CHAINS)
    ones_rows = jnp.ones((BF16_SUBLANES, blk), BF16)

    def col_max(st):
        part = jnp.max(st.reshape(blk // 16, 16, blk), axis=0)
        return jnp.max(part, axis=0, keepdims=True)

    def scores_into(slot, j, which):
        off = pl.multiple_of(j * blk, blk)
        k_aug = jnp.concatenate([k_ref[pl.ds(off, blk), :], kx_ref[pl.ds(off, blk), :]], axis=1)
        for u in which:
            st = jnp.dot(k_aug, qt_scr[:, u * blk:(u + 1) * blk], preferred_element_type=F32)
            st_scr[slot, u] = st
            mx_scr[slot, u] = col_max(st)

    def consume(slot, j, carry, which, diagonal_chain=None):
        vt = jnp.concatenate([vt_ref[0, 0, j], ones_rows], axis=0)
        carry = list(carry)
        for u in which:
            m, acc = carry[u]
            st = st_scr[slot, u]
            if u == diagonal_chain:
                key = lax.broadcasted_iota(jnp.int32, st.shape, 0)
                qry = lax.broadcasted_iota(jnp.int32, st.shape, 1)
                st = jnp.where(key <= qry, st, -1e30)
                mx = col_max(st)
            else:
                mx = mx_scr[slot, u]
            m_new = jnp.maximum(m, mx)
            p = jnp.exp2(st - m_new).astype(BF16)
            acc = jnp.exp2(m - m_new) * acc + jnp.dot(vt, p, preferred_element_type=F32)
            carry[u] = (m_new, acc)
        return tuple(carry)

    def pair(i, carry):
        j0 = 2 * i
        scores_into(1, j0 + 1, chains)
        carry = consume(0, j0, carry, chains)
        scores_into(0, j0 + 2, chains)
        return consume(1, j0 + 1, carry, chains)

    init = tuple((jnp.full((1, blk), -1e30, F32), jnp.zeros((HEAD_DIM + BF16_SUBLANES, blk), F32))
                 for _ in chains)
    scores_into(0, 0, chains)
    carry = lax.fori_loop(0, qi, pair, init)
    scores_into(1, 2 * qi + 1, [1])
    carry = consume(0, 2 * qi, carry, chains, diagonal_chain=0)
    carry = consume(1, 2 * qi + 1, carry, [1], diagonal_chain=1)
    for u in chains:
        acc = carry[u][1]
        out = acc[:HEAD_DIM, :] / acc[HEAD_DIM:HEAD_DIM + 1, :]
        o_ref[u * blk:(u + 1) * blk, :] = out.T.astype(o_ref.dtype)


def _fox_bias_lanes(pieces, bsz, s):
    p = pieces.astype(BF16).transpose(1, 3, 2, 0)
    ones = jnp.ones_like(p)
    pad = jnp.zeros((bsz, s, N_HEADS, FOX_BIAS_LANES - 6), BF16)
    qx = jnp.concatenate([p, ones, pad], axis=-1).reshape(bsz * s, LANES)
    kx = jnp.concatenate([ones, -p, pad], axis=-1).reshape(bsz * s, LANES)
    return qx, kx


def _fox_attention(proj, qx, kx, bsz, s):
    t = proj.shape[0]
    blk = 512
    tq = FOX_CHAINS * blk
    assert s % tq == 0
    nb, nq = s // blk, s // tq
    fox_w = N_HEADS * HEAD_DIM
    vt = proj[:, 2 * fox_w:3 * fox_w].reshape(bsz, nb, blk, N_HEADS, HEAD_DIM).transpose(0, 3, 1, 4, 2)
    pipelined = 3 * _nbytes((s, HEAD_DIM), BF16) + 3 * _nbytes((tq, HEAD_DIM), BF16)
    resident = (2 + 4) * FOX_CHAINS * _nbytes((blk, blk), F32)
    return pl.pallas_call(
        functools.partial(_fox_kernel, blk=blk),
        grid=(bsz, N_HEADS, nq),
        in_specs=[
            pl.BlockSpec((tq, HEAD_DIM), lambda b, h, i: (b * nq + i, h)),
            pl.BlockSpec((tq, LANES), lambda b, h, i: (b * nq + i, 0)),
            pl.BlockSpec((s, HEAD_DIM), lambda b, h, i: (b, N_HEADS + h)),
            pl.BlockSpec((s, LANES), lambda b, h, i: (b, 0)),
            pl.BlockSpec((1, 1, nb, HEAD_DIM, blk), lambda b, h, i: (b, h, 0, 0, 0)),
        ],
        out_specs=pl.BlockSpec((tq, HEAD_DIM), lambda b, h, i: (b * nq + i, h)),
        out_shape=jax.ShapeDtypeStruct((t, fox_w), BF16),
        scratch_shapes=[pltpu.VMEM((2 * HEAD_DIM, tq), BF16),
                        pltpu.VMEM((2, FOX_CHAINS, blk, blk), F32),
                        pltpu.VMEM((2, FOX_CHAINS, 1, blk), F32)],
        compiler_params=pltpu.CompilerParams(
            dimension_semantics=("parallel", "parallel", "arbitrary"),
            vmem_limit_bytes=_vmem_limit(pipelined, resident)),
        name="fox_attention",
    )(proj, qx, proj, kx, vt)


def _retention_tables(chunk):
    h = np.arange(N_HEADS, dtype=np.float64)
    log_g = np.log1p(-np.exp2(-5.0 - h))
    pos = np.arange(chunk, dtype=np.float64)
    rel = pos[:, None] - pos[None, :]
    intra = np.where(rel >= 0, np.exp(log_g[:, None, None] * np.maximum(rel, 0.0)), 0.0)
    q_dec = np.exp(log_g[:, None] * (pos + 1.0))[..., None] * np.ones((1, 1, LANES))
    k_dec = np.exp(log_g[:, None] * (chunk - 1.0 - pos))[..., None] * np.ones((1, 1, LANES))
    c_dec = np.exp(log_g * chunk)[:, None, None] * np.ones((1, 1, LANES))
    return tuple(jnp.asarray(a, F32) for a in (intra, q_dec, k_dec, c_dec))


def _rope(t, cos, sin):
    return t * cos + pltpu.roll(t, HEAD_DIM // 2, 1) * sin


def _retention_kernel(q_ref, k_ref, v_ref, g_ref, cos_ref, sin_ref, intra_ref, qd_ref, kd_ref, cd_ref,
                      gr_ref, o_ref, state_scr, *, chunk):
    @pl.when(pl.program_id(2) == 0)
    def _():
        state_scr[...] = jnp.zeros_like(state_scr)

    intra, q_dec, k_dec, c_dec = intra_ref[0], qd_ref[0], kd_ref[0], cd_ref[0]
    state = state_scr[...]
    for c in range(q_ref.shape[0] // chunk):
        rows = slice(c * chunk, (c + 1) * chunk)
        cos, sin = cos_ref[rows, :], sin_ref[rows, :]
        q = _rope(q_ref[rows, :].astype(F32), cos, sin)
        k = _rope(k_ref[rows, :].astype(F32), cos, sin) * (HEAD_DIM ** -0.5)
        v = v_ref[rows, :]
        sc = lax.dot_general(q.astype(BF16), k.astype(BF16), (((1,), (1,)), ((), ())),
                             preferred_element_type=F32) * intra
        o = jnp.dot(sc.astype(BF16), v, preferred_element_type=F32)
        o += jnp.dot((q * q_dec).astype(BF16), state.astype(BF16), preferred_element_type=F32)
        state = c_dec * state + lax.dot_general((k * k_dec).astype(BF16), v, (((0,), (0,)), ((), ())),
                                                preferred_element_type=F32)
        y = o - jnp.mean(o, axis=-1, keepdims=True)
        y = y * lax.rsqrt(jnp.mean(y * y, axis=-1, keepdims=True) + EPS)
        gate = g_ref[rows, :].astype(F32)
        o_ref[rows, :] = (y * gr_ref[...] * (gate * _sigmoid(gate))).astype(o_ref.dtype)
    state_scr[...] = state


def _retention(proj, col0, g_ret, cos_t, sin_t, bsz, s):
    t = proj.shape[0]
    chunk = min(RET_CHUNK, s)
    lb = min(1024, s)
    nb = s // lb
    intra, q_dec, k_dec, c_dec = _retention_tables(chunk)

    def col(group):
        return pl.BlockSpec((lb, HEAD_DIM), lambda b, h, i: (b * nb + i, col0 + group * N_HEADS + h))

    def table(shape):
        return pl.BlockSpec((1,) + shape, lambda b, h, i: (h, 0, 0))

    pipelined = 5 * _nbytes((lb, HEAD_DIM), BF16) + 2 * _nbytes((lb, HEAD_DIM), F32)
    resident = 2 * (_nbytes((chunk, chunk), F32) + 2 * _nbytes((chunk, LANES), F32)) + 8 * _nbytes((chunk, chunk), F32)
    return pl.pallas_call(
        functools.partial(_retention_kernel, chunk=chunk),
        grid=(bsz, N_HEADS, nb),
        in_specs=[
            col(0), col(1), col(2), col(3),
            pl.BlockSpec((lb, HEAD_DIM), lambda b, h, i: (i, 0)),
            pl.BlockSpec((lb, HEAD_DIM), lambda b, h, i: (i, 0)),
            table((chunk, chunk)), table((chunk, LANES)), table((chunk, LANES)), table((1, LANES)),
            pl.BlockSpec((1, HEAD_DIM), lambda b, h, i: (0, h)),
        ],
        out_specs=pl.BlockSpec((lb, HEAD_DIM), lambda b, h, i: (b * nb + i, h)),
        out_shape=jax.ShapeDtypeStruct((t, N_HEADS * HEAD_DIM), BF16),
        scratch_shapes=[pltpu.VMEM((HEAD_DIM, HEAD_DIM), F32)],
        compiler_params=pltpu.CompilerParams(
            dimension_semantics=("parallel", "parallel", "arbitrary"),
            vmem_limit_bytes=_vmem_limit(pipelined, resident)),
        name="retention",
    )(proj, proj, proj, proj, cos_t, sin_t, intra, q_dec, k_dec, c_dec, g_ret.reshape(1, -1))


def _conv_silu(cur, tail, w):
    row8 = lax.broadcasted_iota(jnp.int32, (8, cur.shape[1]), 0)
    out = cur * w[CONV_W - 1:CONV_W, :]
    for shift in range(1, CONV_W):
        rolled = pltpu.roll(cur, shift, 0)
        head = jnp.where(row8 < shift, pltpu.roll(tail, shift, 0), rolled[:8, :])
        shifted = jnp.concatenate([head, rolled[8:, :]], axis=0)
        out += shifted * w[CONV_W - 1 - shift:CONV_W - shift, :]
    return out * _sigmoid(out)


def _split3(x):
    hi = x.astype(BF16).astype(F32)
    rest = x - hi
    mid = rest.astype(BF16).astype(F32)
    return hi, mid, (rest - mid).astype(BF16).astype(F32)


def _rows16(assign):
    sub = lax.broadcasted_iota(jnp.int32, (BF16_SUBLANES, LANES), 0)
    out = jnp.zeros((BF16_SUBLANES, LANES), F32)
    for r, val in assign.items():
        out = jnp.where(sub == r, val, out)
    return out


def _mlstm_kernel(q_ref, k_ref, v_ref, og_ref, wq_ref, wk_ref, a_ref, b_ref, am_ref, gh_ref, o_ref,
                  c_scr, m_scr, tail_scr, *, chunk):
    @pl.when(pl.program_id(2) == 0)
    def _():
        c_scr[...] = jnp.zeros_like(c_scr)
        m_scr[...] = jnp.zeros_like(m_scr)
        tail_scr[...] = jnp.zeros_like(tail_scr)

    dv = MLSTM_V
    causal = (lax.broadcasted_iota(jnp.int32, (chunk, chunk), 1)
              <= lax.broadcasted_iota(jnp.int32, (chunk, chunk), 0))
    ones_cols = jnp.ones((chunk, LANES), BF16)
    mean_cols = jnp.ones((dv, LANES), BF16)
    rep_g = _rows16({0: 1.0, 1: 1.0, 2: 1.0})
    rep_b = _rows16({3: 1.0, 4: 1.0, 5: 1.0})
    rep_a = _rows16({6: 1.0, 7: 1.0, 8: 1.0})

    chunks = range(q_ref.shape[0] // chunk)
    halves = (slice(0, LANES), slice(LANES, dv))

    def rows(c):
        return slice(c * chunk, (c + 1) * chunk)

    tail = tail_scr[...]
    w_qk = jnp.concatenate([wq_ref[...], wk_ref[...]], axis=1)
    q_b, k_f, sc_raw, v_aug, b_row, a_row, a_cummax, b_last, lw_max = ([] for _ in range(9))
    for c in chunks:
        cur = jnp.concatenate([q_ref[rows(c), :], k_ref[rows(c), :]], axis=1).astype(F32)
        qk = _conv_silu(cur, tail, w_qk)
        tail = cur[chunk - 8:, :]
        q_b.append(qk[:, :MLSTM_QK].astype(BF16))
        k_f.append(qk[:, MLSTM_QK:] * (MLSTM_QK ** -0.5))
        sc_raw.append(lax.dot_general(q_b[c], k_f[c].astype(BF16), (((1,), (1,)), ((), ())),
                                      preferred_element_type=F32))
        v_aug.append(jnp.concatenate([v_ref[rows(c), :], ones_cols], axis=1))
        b_row.append(b_ref[0, :, rows(c)])
        a_row.append(a_ref[0, :, rows(c)])
        a_cummax.append(am_ref[0, :, rows(c)])
        b_last.append(b_row[c][:, chunk - 1:chunk])
        lw_max.append(b_last[c] + a_cummax[c][:, chunk - 1:chunk])
    tail_scr[...] = tail

    m_in, m_out = [], []
    m_st = m_scr[0:1, 0:1]
    for c in chunks:
        m_in.append(m_st)
        m_st = jnp.maximum(b_last[c] + m_st, lw_max[c])
        m_out.append(m_st)
    m_scr[...] = jnp.broadcast_to(m_st, m_scr.shape)

    rep = []
    for c in chunks:
        g3, b3, a3 = _split3(jnp.maximum(a_cummax[c], m_in[c])), _split3(b_row[c]), _split3(a_row[c])
        lhs = _rows16({0: g3[0], 1: g3[1], 2: g3[2], 3: b3[0], 4: b3[1], 5: b3[2],
                       6: a3[0], 7: a3[1], 8: a3[2], 9: 1.0, 10: 1.0, 11: 1.0})
        rep_e = _rows16({0: -1.0, 1: -1.0, 2: -1.0, 9: a3[0], 10: a3[1], 11: a3[2]})
        rhs = jnp.concatenate([rep_e, rep_g, rep_b, rep_a], axis=1)
        rep.append(lax.dot_general(lhs.astype(BF16), rhs.astype(BF16), (((0,), (0,)), ((), ())),
                                   preferred_element_type=F32))

    num, inc = [], []
    for c in chunks:
        e_mat, a_rep = rep[c][:, 0:LANES], rep[c][:, 3 * LANES:]
        sc = jnp.where(causal, sc_raw[c] * jnp.exp(e_mat), 0.0)
        num.append(jnp.dot(sc.astype(BF16), v_aug[c], preferred_element_type=F32))
        kw = k_f[c] * jnp.exp(b_last[c] + a_rep - m_out[c])
        inc.append(lax.dot_general(kw.astype(BF16), v_aug[c], (((0,), (0,)), ((), ())),
                                   preferred_element_type=F32))

    c_in = []
    c_aug = c_scr[...]
    for c in chunks:
        c_in.append(c_aug.astype(BF16))
        c_aug = jnp.exp(b_last[c] + m_in[c] - m_out[c]) * c_aug + inc[c]
    c_scr[...] = c_aug

    hid, sq_sum = [], []
    for c in chunks:
        g_rep, b_rep = rep[c][:, LANES:2 * LANES], rep[c][:, 2 * LANES:3 * LANES]
        w_inter = jnp.exp(m_in[c] - g_rep)
        inter = jnp.dot(q_b[c], c_in[c], preferred_element_type=F32)
        den = num[c][:, dv:] + w_inter * inter[:, dv:]
        inv = 1.0 / jnp.maximum(jnp.abs(den), jnp.exp(-(b_rep + g_rep)))
        hid.append([(num[c][:, sl] + w_inter * inter[:, sl]) * inv for sl in halves])
        sq = jnp.concatenate([hh * hh for hh in hid[c]], axis=1).astype(BF16)
        sq_sum.append(jnp.dot(sq, mean_cols, preferred_element_type=F32))

    for c in chunks:
        scale = lax.rsqrt(sq_sum[c] * (1.0 / dv) + EPS)
        for part, sl in zip(hid[c], halves):
            gate = _sigmoid(og_ref[rows(c), sl].astype(F32))
            o_ref[rows(c), sl] = (part * scale * gh_ref[:, sl] * gate).astype(o_ref.dtype)


def _mlstm(proj, conv_w, gate_rows, g_h, bsz, s):
    t = proj.shape[0]
    chunk = MLSTM_CHUNK
    assert chunk == LANES and MLSTM_QK == LANES
    lb = min(512, s)
    nb = s // lb
    vcol = 2 * N_HEADS * MLSTM_QK // MLSTM_V
    gates = pl.BlockSpec((1, 1, lb), lambda b, h, i: (b * N_HEADS + h, 0, i))
    pipelined = 2 * _nbytes((lb, MLSTM_QK), BF16) + 3 * _nbytes((lb, MLSTM_V), BF16)
    resident = 2 * _nbytes((MLSTM_QK, MLSTM_V + LANES), F32) + 16 * _nbytes((chunk, MLSTM_V + LANES), F32)
    return pl.pallas_call(
        functools.partial(_mlstm_kernel, chunk=chunk),
        grid=(bsz, N_HEADS, nb),
        in_specs=[
            pl.BlockSpec((lb, MLSTM_QK), lambda b, h, i: (b * nb + i, h)),
            pl.BlockSpec((lb, MLSTM_QK), lambda b, h, i: (b * nb + i, N_HEADS + h)),
            pl.BlockSpec((lb, MLSTM_V), lambda b, h, i: (b * nb + i, vcol + h)),
            pl.BlockSpec((lb, MLSTM_V), lambda b, h, i: (b * nb + i, vcol + N_HEADS + h)),
            pl.BlockSpec((CONV_W, MLSTM_QK), lambda b, h, i: (0, h)),
            pl.BlockSpec((CONV_W, MLSTM_QK), lambda b, h, i: (0, N_HEADS + h)),
            gates, gates, gates,
            pl.BlockSpec((1, MLSTM_V), lambda b, h, i: (0, h)),
        ],
        out_specs=pl.BlockSpec((lb, MLSTM_V), lambda b, h, i: (b * nb + i, h)),
        out_shape=jax.ShapeDtypeStruct((t, N_HEADS * MLSTM_V), BF16),
        scratch_shapes=[
            pltpu.VMEM((MLSTM_QK, MLSTM_V + LANES), F32),
            pltpu.VMEM((8, LANES), F32),
            pltpu.VMEM((8, 2 * MLSTM_QK), F32),
        ],
        compiler_params=pltpu.CompilerParams(
            dimension_semantics=("parallel", "parallel", "arbitrary"),
            vmem_limit_bytes=_vmem_limit(pipelined, resident)),
        name="mlstm",
    )(proj, proj, proj, proj, conv_w, conv_w,
      *(r.reshape(bsz * N_HEADS, 1, s) for r in gate_rows), g_h.reshape(1, -1))


XATTN_ROW_PARTS = 2


def _xattn_out_kernel(q_ref, k_ref, v_ref, w_ref, x_ref, o_ref):
    tm, d = q_ref.shape
    dh = d // N_XATTN
    part = tm // XATTN_ROW_PARTS
    heads = [slice(h * dh, (h + 1) * dh) for h in range(N_XATTN)]
    parts = [slice(r * part, (r + 1) * part) for r in range(XATTN_ROW_PARTS)]
    scores = {(r, h): lax.dot_general(q_ref[parts[r], heads[h]], k_ref[:, heads[h]], (((1,), (1,)), ((), ())),
                                      preferred_element_type=F32)
              for r in range(XATTN_ROW_PARTS) for h in range(N_XATTN)}
    for r in range(XATTN_ROW_PARTS):
        outs = []
        for h in range(N_XATTN):
            s = scores[(r, h)]
            p = jnp.exp(s - jnp.max(s, axis=-1, keepdims=True))
            p = p / jnp.sum(p, axis=-1, keepdims=True)
            outs.append(jnp.dot(p.astype(BF16), v_ref[:, heads[h]], preferred_element_type=F32).astype(BF16))
        attn = jnp.concatenate(outs, axis=1)
        o_ref[parts[r], :] = x_ref[parts[r], :] + jnp.dot(attn, w_ref[...], preferred_element_type=F32)


def _xattn_out(q, kv, w, x, s, n_mem):
    t, d = q.shape
    tm = min(512, s)
    per_seq = s // tm
    pipelined = (_nbytes((tm, d), BF16) + 2 * _nbytes((n_mem, d), BF16) + _nbytes((d, d), BF16)
                 + 2 * _nbytes((tm, d), F32))
    resident = (N_XATTN * XATTN_ROW_PARTS + 4) * _nbytes((tm // XATTN_ROW_PARTS, n_mem), F32) + _nbytes((tm, d), F32)
    return pl.pallas_call(
        _xattn_out_kernel,
        grid=(t // tm,),
        in_specs=[
            pl.BlockSpec((tm, d), lambda i: (i, 0)),
            pl.BlockSpec((n_mem, d), lambda i: (i // per_seq, 0)),
            pl.BlockSpec((n_mem, d), lambda i: (i // per_seq, 1)),
            pl.BlockSpec((d, d), lambda i: (0, 0)),
            pl.BlockSpec((tm, d), lambda i: (i, 0)),
        ],
        out_specs=pl.BlockSpec((tm, d), lambda i: (i, 0)),
        out_shape=jax.ShapeDtypeStruct((t, d), F32),
        compiler_params=pltpu.CompilerParams(
            dimension_semantics=("parallel",),
            vmem_limit_bytes=_vmem_limit(pipelined, resident)),
        name="xattn_out",
    )(q, kv, kv, w, x)


def _gate_weight(w_cols):
    return jnp.pad(w_cols, ((0, 0), (0, LANES - w_cols.shape[1]))).astype(BF16)


def _rows(gate_cols, bsz, s):
    return gate_cols.reshape(bsz, s, -1).transpose(0, 2, 1)


def _rope_tables(s):
    inv = 1.0 / (ROPE_BASE ** (jnp.arange(0, HEAD_DIM, 2, dtype=F32) / HEAD_DIM))
    ang = jnp.arange(s, dtype=F32)[:, None] * inv[None, :]
    cos, sin = jnp.cos(ang), jnp.sin(ang)
    return jnp.concatenate([cos, cos], axis=-1), jnp.concatenate([-sin, sin], axis=-1)


def _even_mixer(x, g, w_in, b_f, g_q, g_k, g_ret, w_out, rope, bsz, s):
    fox_w = N_HEADS * HEAD_DIM
    w_main = jnp.concatenate([w_in[:, :3 * fox_w], w_in[:, 3 * fox_w + N_HEADS:]], axis=1).astype(BF16)
    w_gate = _gate_weight(w_in[:, 3 * fox_w:3 * fox_w + N_HEADS])
    qk_gain = jnp.concatenate([jnp.tile(g_q * (HEAD_DIM ** -0.5 * LOG2E), N_HEADS), jnp.tile(g_k, N_HEADS)])
    proj, gate = _norm_matmul(x, g, w_main, w_gate=w_gate, group=HEAD_DIM, group_gain=qk_gain,
                              n_norm_cols=2 * fox_w)
    qx, kx = _fox_bias_lanes(_fox_gates(_rows(gate[:, :N_HEADS], bsz, s), b_f), bsz, s)
    ya = _fox_attention(proj, qx, kx, bsz, s)
    yb = _retention(proj, 3 * N_HEADS, g_ret, rope[0], rope[1], bsz, s)
    return _out_proj(ya, 0, yb, 0, w_out.astype(BF16), x)


def _odd_mixer(x, g, w_in, conv_w, b_i, b_f, g_h, w_out, bsz, s):
    n_main = 2 * N_HEADS * MLSTM_QK + 2 * N_HEADS * MLSTM_V
    proj, gate = _norm_matmul(x, g, w_in[:, :n_main].astype(BF16), w_gate=_gate_weight(w_in[:, n_main:]))
    gate_rows = _mlstm_gates(_rows(gate[:, :N_HEADS], bsz, s),
                             _rows(gate[:, N_HEADS:2 * N_HEADS], bsz, s), b_i, b_f)
    y = _mlstm(proj, conv_w, gate_rows, g_h, bsz, s)
    return _out_proj(y, 0, y, 1, w_out.astype(BF16), x)


def _cross_attention(x, mem, g_x, g_m, wq, wk, wv, wo, g_q, g_k, s):
    d = x.shape[1]
    dh = d // N_XATTN
    n_mem = mem.shape[0] * s // x.shape[0]
    kv = _norm_matmul(mem, g_m, jnp.concatenate([wk, wv], axis=1).astype(BF16),
                      group=dh, group_gain=jnp.tile(g_k, N_XATTN), n_norm_cols=d)
    q = _norm_matmul(x, g_x, wq.astype(BF16), group=dh, group_gain=jnp.tile(g_q * dh ** -0.5, N_XATTN),
                     n_norm_cols=d)
    return _xattn_out(q, kv, wo.astype(BF16), x, s, n_mem)


def kernel(x, mem, norm_mix, norm_xattn, norm_mem, norm_ffn, ev_w_in, ev_b_f, ev_g_q, ev_g_k, ev_g_ret, ev_w_out, od_w_in, od_conv, od_b_i, od_b_f, od_g_h, od_w_out, xa_wq, xa_wk, xa_wv, xa_wo, xa_g_q, xa_g_k, ffn_w1, ffn_w2):
    bsz, s, d = x.shape
    depth = norm_mix.shape[0]
    xf = x.reshape(bsz * s, d)
    memf = mem.reshape(-1, d)
    rope = _rope_tables(s)
    for l in range(depth):
        if l % 2 == 0:
            e = l // 2
            xf = _even_mixer(xf, norm_mix[l], ev_w_in[e], ev_b_f[e], ev_g_q[e], ev_g_k[e], ev_g_ret[e],
                             ev_w_out[e], rope, bsz, s)
        else:
            o = l // 2
            xf = _odd_mixer(xf, norm_mix[l], od_w_in[o], od_conv[o], od_b_i[o], od_b_f[o], od_g_h[o],
                            od_w_out[o], bsz, s)
        xf = _cross_attention(xf, memf, norm_xattn[l], norm_mem[l], xa_wq[l], xa_wk[l], xa_wv[l], xa_wo[l],
                              xa_g_q[l], xa_g_k[l], s)
        xf = _mlp(xf, norm_ffn[l], ffn_w1[l].astype(BF16), ffn_w2[l].astype(BF16))
    return xf.reshape(bsz, s, d)
```

```python
import functools

import numpy as np
import jax
import jax.numpy as jnp
from jax import lax
from jax.experimental import pallas as pl
from jax.experimental.pallas import tpu as pltpu

F32 = jnp.float32
BF16 = jnp.bfloat16

EPS = 1e-6
LOG2E = 1.4426950408889634
LANES = 128
BF16_SUBLANES = 16
HEAD_DIM = 128
N_HEADS = 8
MLSTM_QK = 128
MLSTM_V = 256
N_XATTN = 4
CONV_W = 4
ROPE_BASE = 10000.0
MLSTM_CHUNK = 128
RET_CHUNK = 256
V7X_VMEM_BUDGET = 60 * 1024 * 1024
NORM_ROW_PARTS = 4


def _vmem_limit(pipelined_bytes, resident_bytes):
    return int(min(2 * pipelined_bytes + resident_bytes + (4 << 20), V7X_VMEM_BUDGET))


def _nbytes(shape, dtype):
    return int(np.prod(shape)) * jnp.dtype(dtype).itemsize


def _rms_rows(x, g):
    return x * lax.rsqrt(jnp.mean(x * x, axis=-1, keepdims=True) + EPS) * g


def _sigmoid(x):
    return 1.0 / (1.0 + jnp.exp(-x))


def _log_sigmoid(x):
    return jnp.minimum(x, 0.0) - jnp.log1p(jnp.exp(-jnp.abs(x)))


def _norm_matmul_kernel(*refs, group, n_norm_tiles, has_gate):
    x_ref, g_ref, w_ref = refs[:3]
    pos = 3
    wg_ref = eg_ref = gate_ref = None
    if has_gate:
        wg_ref = refs[pos]
        pos += 1
    if group:
        eg_ref = refs[pos]
        pos += 1
    o_ref = refs[pos]
    pos += 1
    if has_gate:
        gate_ref = refs[pos]
        pos += 1
    h_scr = refs[pos]

    j = pl.program_id(1)
    tm, tn = o_ref.shape

    def store(acc, rows, normed):
        if normed:
            for c in range(tn // group):
                sl = slice(c * group, (c + 1) * group)
                o_ref[rows, sl] = _rms_rows(acc[:, sl], eg_ref[:, sl]).astype(o_ref.dtype)
        else:
            o_ref[rows, :] = acc.astype(o_ref.dtype)

    @pl.when(j == 0)
    def _():
        part = tm // NORM_ROW_PARTS
        for r in range(NORM_ROW_PARTS):
            rows = slice(r * part, (r + 1) * part)
            h = _rms_rows(x_ref[rows, :], g_ref[...]).astype(BF16)
            h_scr[rows, :] = h
            if has_gate:
                gate_ref[rows, :] = jnp.dot(h, wg_ref[...], preferred_element_type=F32)
            store(jnp.dot(h, w_ref[...], preferred_element_type=F32), rows, n_norm_tiles > 0)

    @pl.when(j > 0)
    def _():
        acc = jnp.dot(h_scr[...], w_ref[...], preferred_element_type=F32)
        if not group:
            store(acc, slice(None), False)
            return

        @pl.when(j < n_norm_tiles)
        def _():
            store(acc, slice(None), True)

        @pl.when(j >= n_norm_tiles)
        def _():
            store(acc, slice(None), False)


def _norm_matmul(x, g, w, *, w_gate=None, group=0, group_gain=None, n_norm_cols=0):
    t, d = x.shape
    n = w.shape[1]
    tm = min(1024, t)
    tn = 2048 if (n % 2048 == 0 and n_norm_cols % 2048 == 0) else 1024
    has_gate = w_gate is not None
    n_norm_tiles = n_norm_cols // tn if group else 0
    in_specs = [
        pl.BlockSpec((tm, d), lambda i, j: (i, 0)),
        pl.BlockSpec((1, d), lambda i, j: (0, 0)),
        pl.BlockSpec((d, tn), lambda i, j: (0, j)),
    ]
    args = [x, g.reshape(1, d), w]
    if has_gate:
        in_specs.append(pl.BlockSpec((d, LANES), lambda i, j: (0, 0)))
        args.append(w_gate)
    if group:
        last = n_norm_tiles - 1
        in_specs.append(pl.BlockSpec((1, tn), lambda i, j: (0, jnp.minimum(j, last))))
        args.append(group_gain.reshape(1, n_norm_cols))
    out_shape = [jax.ShapeDtypeStruct((t, n), BF16)]
    out_specs = [pl.BlockSpec((tm, tn), lambda i, j: (i, j))]
    if has_gate:
        out_shape.append(jax.ShapeDtypeStruct((t, LANES), F32))
        out_specs.append(pl.BlockSpec((tm, LANES), lambda i, j: (i, 0)))
    pipelined = (_nbytes((tm, d), F32) + _nbytes((d, tn), BF16) + _nbytes((tm, tn), BF16)
                 + _nbytes((tm, LANES), F32) + _nbytes((d, LANES), BF16))
    resident = _nbytes((tm, d), BF16) + 3 * _nbytes((tm, tn), F32) + _nbytes((tm, d), F32)
    outs = pl.pallas_call(
        functools.partial(_norm_matmul_kernel, group=group, n_norm_tiles=n_norm_tiles, has_gate=has_gate),
        grid=(t // tm, n // tn),
        in_specs=in_specs,
        out_specs=out_specs,
        out_shape=out_shape,
        scratch_shapes=[pltpu.VMEM((tm, d), BF16)],
        compiler_params=pltpu.CompilerParams(
            dimension_semantics=("parallel", "arbitrary"),
            vmem_limit_bytes=_vmem_limit(pipelined, resident)),
        name="norm_matmul",
    )(*args)
    return outs if has_gate else outs[0]


def _out_proj_kernel(al_ref, ar_ref, w_ref, x_ref, o_ref):
    kl = al_ref.shape[1]
    acc = jnp.dot(al_ref[...], w_ref[:kl, :], preferred_element_type=F32)
    acc += jnp.dot(ar_ref[...], w_ref[kl:, :], preferred_element_type=F32)
    o_ref[...] = x_ref[...] + acc


def _out_proj(a_left, col_left, a_right, col_right, w, x):
    t, d = x.shape
    k = w.shape[0]
    kh = k // 2
    tm = min(512, t)
    pipelined = 2 * _nbytes((tm, kh), BF16) + 2 * _nbytes((tm, d), F32) + _nbytes((k, d), BF16)
    return pl.pallas_call(
        _out_proj_kernel,
        grid=(t // tm,),
        in_specs=[
            pl.BlockSpec((tm, kh), lambda i: (i, col_left)),
            pl.BlockSpec((tm, kh), lambda i: (i, col_right)),
            pl.BlockSpec((k, d), lambda i: (0, 0)),
            pl.BlockSpec((tm, d), lambda i: (i, 0)),
        ],
        out_specs=pl.BlockSpec((tm, d), lambda i: (i, 0)),
        out_shape=jax.ShapeDtypeStruct((t, d), F32),
        compiler_params=pltpu.CompilerParams(
            dimension_semantics=("parallel",),
            vmem_limit_bytes=_vmem_limit(pipelined, _nbytes((tm, d), F32))),
        name="out_proj",
    )(a_left, a_right, w, x)


def _mlp_kernel(x_ref, g_ref, w1_ref, w2_ref, o_ref, h_scr):
    def hidden_out(h):
        u = jnp.maximum(jnp.dot(h, w1_ref[...], preferred_element_type=F32), 0.0)
        return jnp.dot((u * u).astype(BF16), w2_ref[...], preferred_element_type=F32)

    @pl.when(pl.program_id(1) == 0)
    def _():
        part = x_ref.shape[0] // NORM_ROW_PARTS
        for r in range(NORM_ROW_PARTS):
            rows = slice(r * part, (r + 1) * part)
            x = x_ref[rows, :]
            h = _rms_rows(x, g_ref[...]).astype(BF16)
            h_scr[rows, :] = h
            o_ref[rows, :] = x + hidden_out(h)

    @pl.when(pl.program_id(1) > 0)
    def _():
        o_ref[...] += hidden_out(h_scr[...])


def _mlp(x, g, w1, w2):
    t, d = x.shape
    f = w1.shape[1]
    tm = min(1024, t)
    tf = min(1024, f)
    pipelined = 2 * _nbytes((tm, d), F32) + 2 * _nbytes((d, tf), BF16)
    resident = _nbytes((tm, d), BF16) + 2 * _nbytes((tm, tf), F32)
    return pl.pallas_call(
        _mlp_kernel,
        grid=(t // tm, f // tf),
        in_specs=[
            pl.BlockSpec((tm, d), lambda i, j: (i, 0)),
            pl.BlockSpec((1, d), lambda i, j: (0, 0)),
            pl.BlockSpec((d, tf), lambda i, j: (0, j)),
            pl.BlockSpec((tf, d), lambda i, j: (j, 0)),
        ],
        out_specs=pl.BlockSpec((tm, d), lambda i, j: (i, 0)),
        out_shape=jax.ShapeDtypeStruct((t, d), F32),
        scratch_shapes=[pltpu.VMEM((tm, d), BF16)],
        compiler_params=pltpu.CompilerParams(
            dimension_semantics=("parallel", "arbitrary"),
            vmem_limit_bytes=_vmem_limit(pipelined, resident)),
        name="mlp",
    )(x, g.reshape(1, d), w1, w2)


def _upper_tri_ones():
    r = lax.broadcasted_iota(jnp.int32, (LANES, LANES), 0)
    c = lax.broadcasted_iota(jnp.int32, (LANES, LANES), 1)
    return jnp.where(r <= c, 1.0, 0.0).astype(BF16)


def _cumsum_lanes(x, tri):
    hi = x.astype(BF16)
    r1 = x - hi.astype(F32)
    mid = r1.astype(BF16)
    lo = (r1 - mid.astype(F32)).astype(BF16)
    out = jnp.dot(hi, tri, preferred_element_type=F32)
    out += jnp.dot(mid, tri, preferred_element_type=F32)
    out += jnp.dot(lo, tri, preferred_element_type=F32)
    return out


def _fox_gate_kernel(f_ref, b_ref, o_ref, carry_scr):
    @pl.when(pl.program_id(1) == 0)
    def _():
        carry_scr[...] = jnp.zeros_like(carry_scr)

    tri = _upper_tri_ones()
    carry = carry_scr[:, 0:1]
    for c in range(f_ref.shape[2] // LANES):
        sl = slice(c * LANES, (c + 1) * LANES)
        ls = _log_sigmoid(f_ref[0, :, sl] + b_ref[...])
        cum = _cumsum_lanes(ls, tri) + carry
        carry = cum[:, LANES - 1:LANES]
        c2 = cum * LOG2E
        hi = c2.astype(BF16).astype(F32)
        mid = (c2 - hi).astype(BF16).astype(F32)
        o_ref[0, 0, :, sl] = hi
        o_ref[1, 0, :, sl] = mid
        o_ref[2, 0, :, sl] = ((c2 - hi) - mid).astype(BF16).astype(F32)
    carry_scr[...] = jnp.broadcast_to(carry, carry_scr.shape)


def _fox_gates(f_rows, b_f):
    bsz, h, s = f_rows.shape
    lb = min(2048, s)
    return pl.pallas_call(
        _fox_gate_kernel,
        grid=(bsz, s // lb),
        in_specs=[pl.BlockSpec((1, h, lb), lambda b, i: (b, 0, i)),
                  pl.BlockSpec((h, 1), lambda b, i: (0, 0))],
        out_specs=pl.BlockSpec((3, 1, h, lb), lambda b, i: (0, b, 0, i)),
        out_shape=jax.ShapeDtypeStruct((3, bsz, h, s), F32),
        scratch_shapes=[pltpu.VMEM((h, LANES), F32)],
        compiler_params=pltpu.CompilerParams(dimension_semantics=("parallel", "arbitrary")),
        name="fox_gates",
    )(f_rows, b_f.reshape(h, 1))


def _prefix_max_lanes(x):
    lane = lax.broadcasted_iota(jnp.int32, x.shape, 1)
    shift = 1
    while shift < LANES:
        x = jnp.maximum(x, jnp.where(lane >= shift, pltpu.roll(x, shift, 1), -jnp.inf))
        shift *= 2
    return x


def _mlstm_gate_kernel(i_ref, f_ref, bi_ref, bf_ref, ao_ref, bo_ref, mo_ref):
    tri = _upper_tri_ones()
    for c in range(f_ref.shape[2] // LANES):
        sl = slice(c * LANES, (c + 1) * LANES)
        b = _cumsum_lanes(_log_sigmoid(f_ref[0, :, sl] + bf_ref[...]), tri)
        a = i_ref[0, :, sl] + bi_ref[...] - b
        bo_ref[0, :, sl] = b
        ao_ref[0, :, sl] = a
        mo_ref[0, :, sl] = _prefix_max_lanes(a)


def _mlstm_gates(i_rows, f_rows, b_i, b_f):
    assert MLSTM_CHUNK == LANES
    bsz, h, s = f_rows.shape
    lb = min(2048, s)
    row = pl.BlockSpec((1, h, lb), lambda b, i: (b, 0, i))
    bias = pl.BlockSpec((h, 1), lambda b, i: (0, 0))
    return pl.pallas_call(
        _mlstm_gate_kernel,
        grid=(bsz, s // lb),
        in_specs=[row, row, bias, bias],
        out_specs=[row, row, row],
        out_shape=[jax.ShapeDtypeStruct((bsz, h, s), F32)] * 3,
        compiler_params=pltpu.CompilerParams(dimension_semantics=("parallel", "parallel")),
        name="mlstm_gates",
    )(i_rows, f_rows, b_i.reshape(h, 1), b_f.reshape(h, 1))


FOX_CHAINS = 2
FOX_BIAS_LANES = LANES // N_HEADS
FOX_SKIP_BITS = 150.0


def _fox_kernel(qkb_ref, fq_ref, fk_ref, q_ref, qx_ref, k_ref, kx_ref, vt_ref, o_ref, qt_scr, st_scr, mx_scr,
                *, blk):
    qi = pl.program_id(2)
    bh = pl.program_id(0) * N_HEADS + pl.program_id(1)
    lane = lax.broadcasted_iota(jnp.int32, qx_ref.shape, 1)
    owner = lax.shift_right_logical(lane, FOX_BIAS_LANES.bit_length() - 1)
    qx = jnp.where(owner == pl.program_id(1), qx_ref[...].astype(F32), 0.0)
    q_aug = jnp.concatenate([q_ref[...].astype(F32), qx], axis=1)
    qt_scr[...] = q_aug.T.astype(BF16)
    chains = range(FOX_CHAINS)
    ones_rows = jnp.ones((BF16_SUBLANES, blk), BF16)

    def col_max(st):
        part = jnp.max(st.reshape(blk // 16, 16, blk), axis=0)
        return jnp.max(part, axis=0, keepdims=True)

    def scores_into(slot, j, which):
        off = pl.multiple_of(j * blk, blk)
        k_aug = jnp.concatenate([k_ref[pl.ds(off, blk), :], kx_ref[pl.ds(off, blk), :]], axis=1)
        for u in which:
            st = jnp.dot(k_aug, qt_scr[:, u * blk:(u + 1) * blk], preferred_element_type=F32)
            st_scr[slot, u] = st
            mx_scr[slot, u] = col_max(st)

    def consume(slot, j, carry, which, diagonal_chain=None):
        vt = jnp.concatenate([vt_ref[0, 0, j], ones_rows], axis=0)
        carry = list(carry)
        for u in which:
            m, acc = carry[u]
            st = st_scr[slot, u]
            if u == diagonal_chain:
                key = lax.broadcasted_iota(jnp.int32, st.shape, 0)
                qry = lax.broadcasted_iota(jnp.int32, st.shape, 1)
                st = jnp.where(key <= qry, st, -1e30)
                mx = col_max(st)
            else:
                mx = mx_scr[slot, u]
            m_new = jnp.maximum(m, mx)
            p = jnp.exp2(st - m_new).astype(BF16)
            acc = jnp.exp2(m - m_new) * acc + jnp.dot(vt, p, preferred_element_type=F32)
            carry[u] = (m_new, acc)
        return tuple(carry)

    f_first_query = fq_ref[bh, qi]
    qkb = qkb_ref[0]

    def needed(a, carry):
        lowest_max = jnp.min(jnp.minimum(carry[0][0], carry[1][0]))
        bound = qkb + f_first_query - fk_ref[bh, jnp.maximum(a, 0)]
        return jnp.logical_and(a >= 1, bound > lowest_max - FOX_SKIP_BITS).astype(jnp.int32)

    def pair(state):
        a, _, carry = state
        scores_into(0, a - 1, chains)
        carry = consume(1, a, carry, chains)
        scores_into(1, jnp.maximum(a - 2, 0), chains)
        carry = consume(0, a - 1, carry, chains)
        return a - 2, needed(a - 2, carry), carry

    init = tuple((jnp.full((1, blk), -1e30, F32), jnp.zeros((HEAD_DIM + BF16_SUBLANES, blk), F32))
                 for _ in chains)
    scores_into(1, 2 * qi + 1, [1])
    scores_into(0, 2 * qi, chains)
    carry = consume(1, 2 * qi + 1, init, [1], diagonal_chain=1)
    older = 2 * qi - 1
    scores_into(1, jnp.maximum(older, 0), chains)
    carry = consume(0, 2 * qi, carry, chains, diagonal_chain=0)
    _, _, carry = lax.while_loop(lambda state: state[1] > 0, pair, (older, needed(older, carry), carry))
    for u in chains:
        acc = carry[u][1]
        out = acc[:HEAD_DIM, :] / acc[HEAD_DIM:HEAD_DIM + 1, :]
        o_ref[u * blk:(u + 1) * blk, :] = out.T.astype(o_ref.dtype)


def _fox_bias_lanes(pieces, bsz, s):
    p = pieces.astype(BF16).transpose(1, 3, 2, 0)
    ones = jnp.ones_like(p)
    pad = jnp.zeros((bsz, s, N_HEADS, FOX_BIAS_LANES - 6), BF16)
    qx = jnp.concatenate([p, ones, pad], axis=-1).reshape(bsz * s, LANES)
    kx = jnp.concatenate([ones, -p, pad], axis=-1).reshape(bsz * s, LANES)
    return qx, kx


def _fox_attention(proj, pieces, qk_bound, bsz, s):
    t = proj.shape[0]
    blk = 512
    tq = FOX_CHAINS * blk
    assert s % tq == 0
    nb, nq = s // blk, s // tq
    fox_w = N_HEADS * HEAD_DIM
    qx, kx = _fox_bias_lanes(pieces, bsz, s)
    f2 = (pieces[0] + pieces[1] + pieces[2]).reshape(bsz * N_HEADS, s)
    f_first_query = f2[:, ::tq]
    f_last_key = f2[:, blk - 1::blk]
    vt = proj[:, 2 * fox_w:3 * fox_w].reshape(bsz, nb, blk, N_HEADS, HEAD_DIM).transpose(0, 3, 1, 4, 2)
    pipelined = 3 * _nbytes((s, HEAD_DIM), BF16) + 3 * _nbytes((tq, HEAD_DIM), BF16)
    resident = (2 + 4) * FOX_CHAINS * _nbytes((blk, blk), F32)
    smem = pl.BlockSpec(memory_space=pltpu.SMEM)
    return pl.pallas_call(
        functools.partial(_fox_kernel, blk=blk),
        grid=(bsz, N_HEADS, nq),
        in_specs=[
            smem, smem, smem,
            pl.BlockSpec((tq, HEAD_DIM), lambda b, h, i: (b * nq + i, h)),
            pl.BlockSpec((tq, LANES), lambda b, h, i: (b * nq + i, 0)),
            pl.BlockSpec((s, HEAD_DIM), lambda b, h, i: (b, N_HEADS + h)),
            pl.BlockSpec((s, LANES), lambda b, h, i: (b, 0)),
            pl.BlockSpec((1, 1, nb, HEAD_DIM, blk), lambda b, h, i: (b, h, 0, 0, 0)),
        ],
        out_specs=pl.BlockSpec((tq, HEAD_DIM), lambda b, h, i: (b * nq + i, h)),
        out_shape=jax.ShapeDtypeStruct((t, fox_w), BF16),
        scratch_shapes=[pltpu.VMEM((2 * HEAD_DIM, tq), BF16),
                        pltpu.VMEM((2, FOX_CHAINS, blk, blk), F32),
                        pltpu.VMEM((2, FOX_CHAINS, 1, blk), F32)],
        compiler_params=pltpu.CompilerParams(
            dimension_semantics=("parallel", "parallel", "arbitrary"),
            vmem_limit_bytes=_vmem_limit(pipelined, resident)),
        name="fox_attention",
    )(qk_bound.reshape(1).astype(F32), f_first_query, f_last_key, proj, qx, proj, kx, vt)


def _retention_tables(chunk):
    h = np.arange(N_HEADS, dtype=np.float64)
    log_g = np.log1p(-np.exp2(-5.0 - h))
    pos = np.arange(chunk, dtype=np.float64)
    rel = pos[:, None] - pos[None, :]
    intra = np.where(rel >= 0, np.exp(log_g[:, None, None] * np.maximum(rel, 0.0)), 0.0)
    q_dec = np.exp(log_g[:, None] * (pos + 1.0))[..., None] * np.ones((1, 1, LANES))
    k_dec = np.exp(log_g[:, None] * (chunk - 1.0 - pos))[..., None] * np.ones((1, 1, LANES))
    c_dec = np.exp(log_g * chunk)[:, None, None] * np.ones((1, 1, LANES))
    return tuple(jnp.asarray(a, F32) for a in (intra, q_dec, k_dec, c_dec))


def _rope(t, cos, sin):
    return t * cos + pltpu.roll(t, HEAD_DIM // 2, 1) * sin


def _retention_kernel(q_ref, k_ref, v_ref, g_ref, cos_ref, sin_ref, intra_ref, qd_ref, kd_ref, cd_ref,
                      gr_ref, o_ref, state_scr, *, chunk):
    @pl.when(pl.program_id(2) == 0)
    def _():
        state_scr[...] = jnp.zeros_like(state_scr)

    intra, q_dec, k_dec, c_dec = intra_ref[0], qd_ref[0], kd_ref[0], cd_ref[0]
    state = state_scr[...]
    for c in range(q_ref.shape[0] // chunk):
        rows = slice(c * chunk, (c + 1) * chunk)
        cos, sin = cos_ref[rows, :], sin_ref[rows, :]
        q = _rope(q_ref[rows, :].astype(F32), cos, sin)
        k = _rope(k_ref[rows, :].astype(F32), cos, sin) * (HEAD_DIM ** -0.5)
        v = v_ref[rows, :]
        sc = lax.dot_general(q.astype(BF16), k.astype(BF16), (((1,), (1,)), ((), ())),
                             preferred_element_type=F32) * intra
        o = jnp.dot(sc.astype(BF16), v, preferred_element_type=F32)
        o += jnp.dot((q * q_dec).astype(BF16), state.astype(BF16), preferred_element_type=F32)
        state = c_dec * state + lax.dot_general((k * k_dec).astype(BF16), v, (((0,), (0,)), ((), ())),
                                                preferred_element_type=F32)
        y = o - jnp.mean(o, axis=-1, keepdims=True)
        y = y * lax.rsqrt(jnp.mean(y * y, axis=-1, keepdims=True) + EPS)
        gate = g_ref[rows, :].astype(F32)
        o_ref[rows, :] = (y * gr_ref[...] * (gate * _sigmoid(gate))).astype(o_ref.dtype)
    state_scr[...] = state


def _retention(proj, col0, g_ret, cos_t, sin_t, bsz, s):
    t = proj.shape[0]
    chunk = min(RET_CHUNK, s)
    lb = min(1024, s)
    nb = s // lb
    intra, q_dec, k_dec, c_dec = _retention_tables(chunk)

    def col(group):
        return pl.BlockSpec((lb, HEAD_DIM), lambda b, h, i: (b * nb + i, col0 + group * N_HEADS + h))

    def table(shape):
        return pl.BlockSpec((1,) + shape, lambda b, h, i: (h, 0, 0))

    pipelined = 5 * _nbytes((lb, HEAD_DIM), BF16) + 2 * _nbytes((lb, HEAD_DIM), F32)
    resident = 2 * (_nbytes((chunk, chunk), F32) + 2 * _nbytes((chunk, LANES), F32)) + 8 * _nbytes((chunk, chunk), F32)
    return pl.pallas_call(
        functools.partial(_retention_kernel, chunk=chunk),
        grid=(bsz, N_HEADS, nb),
        in_specs=[
            col(0), col(1), col(2), col(3),
            pl.BlockSpec((lb, HEAD_DIM), lambda b, h, i: (i, 0)),
            pl.BlockSpec((lb, HEAD_DIM), lambda b, h, i: (i, 0)),
            table((chunk, chunk)), table((chunk, LANES)), table((chunk, LANES)), table((1, LANES)),
            pl.BlockSpec((1, HEAD_DIM), lambda b, h, i: (0, h)),
        ],
        out_specs=pl.BlockSpec((lb, HEAD_DIM), lambda b, h, i: (b * nb + i, h)),
        out_shape=jax.ShapeDtypeStruct((t, N_HEADS * HEAD_DIM), BF16),
        scratch_shapes=[pltpu.VMEM((HEAD_DIM, HEAD_DIM), F32)],
        compiler_params=pltpu.CompilerParams(
            dimension_semantics=("parallel", "parallel", "arbitrary"),
            vmem_limit_bytes=_vmem_limit(pipelined, resident)),
        name="retention",
    )(proj, proj, proj, proj, cos_t, sin_t, intra, q_dec, k_dec, c_dec, g_ret.reshape(1, -1))


def _conv_silu(cur, tail, w):
    row8 = lax.broadcasted_iota(jnp.int32, (8, cur.shape[1]), 0)
    out = cur * w[CONV_W - 1:CONV_W, :]
    for shift in range(1, CONV_W):
        rolled = pltpu.roll(cur, shift, 0)
        head = jnp.where(row8 < shift, pltpu.roll(tail, shift, 0), rolled[:8, :])
        shifted = jnp.concatenate([head, rolled[8:, :]], axis=0)
        out += shifted * w[CONV_W - 1 - shift:CONV_W - shift, :]
    return out * _sigmoid(out)


def _split3(x):
    hi = x.astype(BF16).astype(F32)
    rest = x - hi
    mid = rest.astype(BF16).astype(F32)
    return hi, mid, (rest - mid).astype(BF16).astype(F32)


def _rows16(assign):
    sub = lax.broadcasted_iota(jnp.int32, (BF16_SUBLANES, LANES), 0)
    out = jnp.zeros((BF16_SUBLANES, LANES), F32)
    for r, val in assign.items():
        out = jnp.where(sub == r, val, out)
    return out


def _mlstm_kernel(q_ref, k_ref, v_ref, og_ref, wq_ref, wk_ref, a_ref, b_ref, am_ref, gh_ref, o_ref,
                  c_scr, m_scr, tail_scr, *, chunk):
    @pl.when(pl.program_id(2) == 0)
    def _():
        c_scr[...] = jnp.zeros_like(c_scr)
        m_scr[...] = jnp.zeros_like(m_scr)
        tail_scr[...] = jnp.zeros_like(tail_scr)

    dv = MLSTM_V
    causal = (lax.broadcasted_iota(jnp.int32, (chunk, chunk), 1)
              <= lax.broadcasted_iota(jnp.int32, (chunk, chunk), 0))
    ones_cols = jnp.ones((chunk, LANES), BF16)
    mean_cols = jnp.ones((dv, LANES), BF16)
    rep_g = _rows16({0: 1.0, 1: 1.0, 2: 1.0})
    rep_b = _rows16({3: 1.0, 4: 1.0, 5: 1.0})
    rep_a = _rows16({6: 1.0, 7: 1.0, 8: 1.0})

    chunks = range(q_ref.shape[0] // chunk)
    halves = (slice(0, LANES), slice(LANES, dv))

    def rows(c):
        return slice(c * chunk, (c + 1) * chunk)

    tail = tail_scr[...]
    w_qk = jnp.concatenate([wq_ref[...], wk_ref[...]], axis=1)
    q_b, k_f, sc_raw, v_aug, b_row, a_row, a_cummax, b_last, lw_max = ([] for _ in range(9))
    for c in chunks:
        cur = jnp.concatenate([q_ref[rows(c), :], k_ref[rows(c), :]], axis=1).astype(F32)
        qk = _conv_silu(cur, tail, w_qk)
        tail = cur[chunk - 8:, :]
        q_b.append(qk[:, :MLSTM_QK].astype(BF16))
        k_f.append(qk[:, MLSTM_QK:] * (MLSTM_QK ** -0.5))
        sc_raw.append(lax.dot_general(q_b[c], k_f[c].astype(BF16), (((1,), (1,)), ((), ())),
                                      preferred_element_type=F32))
        v_aug.append(jnp.concatenate([v_ref[rows(c), :], ones_cols], axis=1))
        b_row.append(b_ref[0, :, rows(c)])
        a_row.append(a_ref[0, :, rows(c)])
        a_cummax.append(am_ref[0, :, rows(c)])
        b_last.append(b_row[c][:, chunk - 1:chunk])
        lw_max.append(b_last[c] + a_cummax[c][:, chunk - 1:chunk])
    tail_scr[...] = tail

    m_in, m_out = [], []
    m_st = m_scr[0:1, 0:1]
    for c in chunks:
        m_in.append(m_st)
        m_st = jnp.maximum(b_last[c] + m_st, lw_max[c])
        m_out.append(m_st)
    m_scr[...] = jnp.broadcast_to(m_st, m_scr.shape)

    rep = []
    for c in chunks:
        g3, b3, a3 = _split3(jnp.maximum(a_cummax[c], m_in[c])), _split3(b_row[c]), _split3(a_row[c])
        lhs = _rows16({0: g3[0], 1: g3[1], 2: g3[2], 3: b3[0], 4: b3[1], 5: b3[2],
                       6: a3[0], 7: a3[1], 8: a3[2], 9: 1.0, 10: 1.0, 11: 1.0})
        rep_e = _rows16({0: -1.0, 1: -1.0, 2: -1.0, 9: a3[0], 10: a3[1], 11: a3[2]})
        rhs = jnp.concatenate([rep_e, rep_g, rep_b, rep_a], axis=1)
        rep.append(lax.dot_general(lhs.astype(BF16), rhs.astype(BF16), (((0,), (0,)), ((), ())),
                                   preferred_element_type=F32))

    num, inc = [], []
    for c in chunks:
        e_mat, a_rep = rep[c][:, 0:LANES], rep[c][:, 3 * LANES:]
        sc = jnp.where(causal, sc_raw[c] * jnp.exp(e_mat), 0.0)
        num.append(jnp.dot(sc.astype(BF16), v_aug[c], preferred_element_type=F32))
        kw = k_f[c] * jnp.exp(b_last[c] + a_rep - m_out[c])
        inc.append(lax.dot_general(kw.astype(BF16), v_aug[c], (((0,), (0,)), ((), ())),
                                   preferred_element_type=F32))

    c_in = []
    c_aug = c_scr[...]
    for c in chunks:
        c_in.append(c_aug.astype(BF16))
        c_aug = jnp.exp(b_last[c] + m_in[c] - m_out[c]) * c_aug + inc[c]
    c_scr[...] = c_aug

    hid, sq_sum = [], []
    for c in chunks:
        g_rep, b_rep = rep[c][:, LANES:2 * LANES], rep[c][:, 2 * LANES:3 * LANES]
        w_inter = jnp.exp(m_in[c] - g_rep)
        inter = jnp.dot(q_b[c], c_in[c], preferred_element_type=F32)
        den = num[c][:, dv:] + w_inter * inter[:, dv:]
        inv = 1.0 / jnp.maximum(jnp.abs(den), jnp.exp(-(b_rep + g_rep)))
        hid.append([(num[c][:, sl] + w_inter * inter[:, sl]) * inv for sl in halves])
        sq = jnp.concatenate([hh * hh for hh in hid[c]], axis=1).astype(BF16)
        sq_sum.append(jnp.dot(sq, mean_cols, preferred_element_type=F32))

    for c in chunks:
        scale = lax.rsqrt(sq_sum[c] * (1.0 / dv) + EPS)
        for part, sl in zip(hid[c], halves):
            gate = _sigmoid(og_ref[rows(c), sl].astype(F32))
            o_ref[rows(c), sl] = (part * scale * gh_ref[:, sl] * gate).astype(o_ref.dtype)


def _mlstm(proj, conv_w, gate_rows, g_h, bsz, s):
    t = proj.shape[0]
    chunk = MLSTM_CHUNK
    assert chunk == LANES and MLSTM_QK == LANES
    lb = min(512, s)
    nb = s // lb
    vcol = 2 * N_HEADS * MLSTM_QK // MLSTM_V
    gates = pl.BlockSpec((1, 1, lb), lambda b, h, i: (b * N_HEADS + h, 0, i))
    pipelined = 2 * _nbytes((lb, MLSTM_QK), BF16) + 3 * _nbytes((lb, MLSTM_V), BF16)
    resident = 2 * _nbytes((MLSTM_QK, MLSTM_V + LANES), F32) + 16 * _nbytes((chunk, MLSTM_V + LANES), F32)
    return pl.pallas_call(
        functools.partial(_mlstm_kernel, chunk=chunk),
        grid=(bsz, N_HEADS, nb),
        in_specs=[
            pl.BlockSpec((lb, MLSTM_QK), lambda b, h, i: (b * nb + i, h)),
            pl.BlockSpec((lb, MLSTM_QK), lambda b, h, i: (b * nb + i, N_HEADS + h)),
            pl.BlockSpec((lb, MLSTM_V), lambda b, h, i: (b * nb + i, vcol + h)),
            pl.BlockSpec((lb, MLSTM_V), lambda b, h, i: (b * nb + i, vcol + N_HEADS + h)),
            pl.BlockSpec((CONV_W, MLSTM_QK), lambda b, h, i: (0, h)),
            pl.BlockSpec((CONV_W, MLSTM_QK), lambda b, h, i: (0, N_HEADS + h)),
            gates, gates, gates,
            pl.BlockSpec((1, MLSTM_V), lambda b, h, i: (0, h)),
        ],
        out_specs=pl.BlockSpec((lb, MLSTM_V), lambda b, h, i: (b * nb + i, h)),
        out_shape=jax.ShapeDtypeStruct((t, N_HEADS * MLSTM_V), BF16),
        scratch_shapes=[
            pltpu.VMEM((MLSTM_QK, MLSTM_V + LANES), F32),
            pltpu.VMEM((8, LANES), F32),
            pltpu.VMEM((8, 2 * MLSTM_QK), F32),
        ],
        compiler_params=pltpu.CompilerParams(
            dimension_semantics=("parallel", "parallel", "arbitrary"),
            vmem_limit_bytes=_vmem_limit(pipelined, resident)),
        name="mlstm",
    )(proj, proj, proj, proj, conv_w, conv_w,
      *(r.reshape(bsz * N_HEADS, 1, s) for r in gate_rows), g_h.reshape(1, -1))


XATTN_ROW_PARTS = 2


def _xattn_out_kernel(q_ref, k_ref, v_ref, w_ref, x_ref, o_ref):
    tm, d = q_ref.shape
    dh = d // N_XATTN
    part = tm // XATTN_ROW_PARTS
    heads = [slice(h * dh, (h + 1) * dh) for h in range(N_XATTN)]
    parts = [slice(r * part, (r + 1) * part) for r in range(XATTN_ROW_PARTS)]
    scores = {(r, h): lax.dot_general(q_ref[parts[r], heads[h]], k_ref[:, heads[h]], (((1,), (1,)), ((), ())),
                                      preferred_element_type=F32)
              for r in range(XATTN_ROW_PARTS) for h in range(N_XATTN)}
    for r in range(XATTN_ROW_PARTS):
        outs = []
        for h in range(N_XATTN):
            s = scores[(r, h)]
            p = jnp.exp(s - jnp.max(s, axis=-1, keepdims=True))
            p = p / jnp.sum(p, axis=-1, keepdims=True)
            outs.append(jnp.dot(p.astype(BF16), v_ref[:, heads[h]], preferred_element_type=F32).astype(BF16))
        attn = jnp.concatenate(outs, axis=1)
        o_ref[parts[r], :] = x_ref[parts[r], :] + jnp.dot(attn, w_ref[...], preferred_element_type=F32)


def _xattn_out(q, kv, w, x, s, n_mem):
    t, d = q.shape
    tm = min(512, s)
    per_seq = s // tm
    pipelined = (_nbytes((tm, d), BF16) + 2 * _nbytes((n_mem, d), BF16) + _nbytes((d, d), BF16)
                 + 2 * _nbytes((tm, d), F32))
    resident = (N_XATTN * XATTN_ROW_PARTS + 4) * _nbytes((tm // XATTN_ROW_PARTS, n_mem), F32) + _nbytes((tm, d), F32)
    return pl.pallas_call(
        _xattn_out_kernel,
        grid=(t // tm,),
        in_specs=[
            pl.BlockSpec((tm, d), lambda i: (i, 0)),
            pl.BlockSpec((n_mem, d), lambda i: (i // per_seq, 0)),
            pl.BlockSpec((n_mem, d), lambda i: (i // per_seq, 1)),
            pl.BlockSpec((d, d), lambda i: (0, 0)),
            pl.BlockSpec((tm, d), lambda i: (i, 0)),
        ],
        out_specs=pl.BlockSpec((tm, d), lambda i: (i, 0)),
        out_shape=jax.ShapeDtypeStruct((t, d), F32),
        compiler_params=pltpu.CompilerParams(
            dimension_semantics=("parallel",),
            vmem_limit_bytes=_vmem_limit(pipelined, resident)),
        name="xattn_out",
    )(q, kv, kv, w, x)


def _gate_weight(w_cols):
    return jnp.pad(w_cols, ((0, 0), (0, LANES - w_cols.shape[1]))).astype(BF16)


def _rows(gate_cols, bsz, s):
    return gate_cols.reshape(bsz, s, -1).transpose(0, 2, 1)


def _rope_tables(s):
    inv = 1.0 / (ROPE_BASE ** (jnp.arange(0, HEAD_DIM, 2, dtype=F32) / HEAD_DIM))
    ang = jnp.arange(s, dtype=F32)[:, None] * inv[None, :]
    cos, sin = jnp.cos(ang), jnp.sin(ang)
    return jnp.concatenate([cos, cos], axis=-1), jnp.concatenate([-sin, sin], axis=-1)


def _even_mixer(x, g, w_in, b_f, g_q, g_k, g_ret, w_out, rope, bsz, s):
    fox_w = N_HEADS * HEAD_DIM
    w_main = jnp.concatenate([w_in[:, :3 * fox_w], w_in[:, 3 * fox_w + N_HEADS:]], axis=1).astype(BF16)
    w_gate = _gate_weight(w_in[:, 3 * fox_w:3 * fox_w + N_HEADS])
    qk_gain = jnp.concatenate([jnp.tile(g_q * (HEAD_DIM ** -0.5 * LOG2E), N_HEADS), jnp.tile(g_k, N_HEADS)])
    proj, gate = _norm_matmul(x, g, w_main, w_gate=w_gate, group=HEAD_DIM, group_gain=qk_gain,
                              n_norm_cols=2 * fox_w)
    pieces = _fox_gates(_rows(gate[:, :N_HEADS], bsz, s), b_f)
    qk_bound = 1.01 * HEAD_DIM ** 0.5 * LOG2E * jnp.max(jnp.abs(g_q)) * jnp.max(jnp.abs(g_k))
    ya = _fox_attention(proj, pieces, qk_bound, bsz, s)
    yb = _retention(proj, 3 * N_HEADS, g_ret, rope[0], rope[1], bsz, s)
    return _out_proj(ya, 0, yb, 0, w_out.astype(BF16), x)


def _odd_mixer(x, g, w_in, conv_w, b_i, b_f, g_h, w_out, bsz, s):
    n_main = 2 * N_HEADS * MLSTM_QK + 2 * N_HEADS * MLSTM_V
    proj, gate = _norm_matmul(x, g, w_in[:, :n_main].astype(BF16), w_gate=_gate_weight(w_in[:, n_main:]))
    gate_rows = _mlstm_gates(_rows(gate[:, :N_HEADS], bsz, s),
                             _rows(gate[:, N_HEADS:2 * N_HEADS], bsz, s), b_i, b_f)
    y = _mlstm(proj, conv_w, gate_rows, g_h, bsz, s)
    return _out_proj(y, 0, y, 1, w_out.astype(BF16), x)


def _cross_attention(x, mem, g_x, g_m, wq, wk, wv, wo, g_q, g_k, s):
    d = x.shape[1]
    dh = d // N_XATTN
    n_mem = mem.shape[0] * s // x.shape[0]
    kv = _norm_matmul(mem, g_m, jnp.concatenate([wk, wv], axis=1).astype(BF16),
                      group=dh, group_gain=jnp.tile(g_k, N_XATTN), n_norm_cols=d)
    q = _norm_matmul(x, g_x, wq.astype(BF16), group=dh, group_gain=jnp.tile(g_q * dh ** -0.5, N_XATTN),
                     n_norm_cols=d)
    return _xattn_out(q, kv, wo.astype(BF16), x, s, n_mem)


def kernel(x, mem, norm_mix, norm_xattn, norm_mem, norm_ffn, ev_w_in, ev_b_f, ev_g_q, ev_g_k, ev_g_ret, ev_w_out, od_w_in, od_conv, od_b_i, od_b_f, od_g_h, od_w_out, xa_wq, xa_wk, xa_wv, xa_wo, xa_g_q, xa_g_k, ffn_w1, ffn_w2):
    bsz, s, d = x.shape
    depth = norm_mix.shape[0]
    xf = x.reshape(bsz * s, d)
    memf = mem.reshape(-1, d)
    rope = _rope_tables(s)
    for l in range(depth):
        if l % 2 == 0:
            e = l // 2
            xf = _even_mixer(xf, norm_mix[l], ev_w_in[e], ev_b_f[e], ev_g_q[e], ev_g_k[e], ev_g_ret[e],
                             ev_w_out[e], rope, bsz, s)
        else:
            o = l // 2
            xf = _odd_mixer(xf, norm_mix[l], od_w_in[o], od_conv[o], od_b_i[o], od_b_f[o], od_g_h[o],
                            od_w_out[o], bsz, s)
        xf = _cross_attention(xf, memf, norm_xattn[l], norm_mem[l], xa_wq[l], xa_wk[l], xa_wv[l], xa_wo[l],
                              xa_g_q[l], xa_g_k[l], s)
        xf = _mlp(xf, norm_ffn[l], ffn_w1[l].astype(BF16), ffn_w2[l].astype(BF16))
    return xf.reshape(bsz, s, d)
```

```python
import functools

import numpy as np
import jax
import jax.numpy as jnp
from jax import lax
from jax.experimental import pallas as pl
from jax.experimental.pallas import tpu as pltpu

F32 = jnp.float32
BF16 = jnp.bfloat16

EPS = 1e-6
LOG2E = 1.4426950408889634
LANES = 128
BF16_SUBLANES = 16
HEAD_DIM = 128
N_HEADS = 8
MLSTM_QK = 128
MLSTM_V = 256
N_XATTN = 4
CONV_W = 4
ROPE_BASE = 10000.0
MLSTM_CHUNK = 128
RET_CHUNK = 256
V7X_VMEM_BUDGET = 60 * 1024 * 1024
NORM_ROW_PARTS = 4


def _vmem_limit(pipelined_bytes, resident_bytes):
    return int(min(2 * pipelined_bytes + resident_bytes + (4 << 20), V7X_VMEM_BUDGET))


def _nbytes(shape, dtype):
    return int(np.prod(shape)) * jnp.dtype(dtype).itemsize


def _rms_rows(x, g):
    return x * lax.rsqrt(jnp.mean(x * x, axis=-1, keepdims=True) + EPS) * g


def _sigmoid(x):
    return 1.0 / (1.0 + jnp.exp(-x))


def _log_sigmoid(x):
    return jnp.minimum(x, 0.0) - jnp.log1p(jnp.exp(-jnp.abs(x)))


def _norm_matmul_kernel(*refs, group, n_norm_tiles, has_gate):
    x_ref, g_ref, w_ref = refs[:3]
    pos = 3
    wg_ref = eg_ref = gate_ref = None
    if has_gate:
        wg_ref = refs[pos]
        pos += 1
    if group:
        eg_ref = refs[pos]
        pos += 1
    o_ref = refs[pos]
    pos += 1
    if has_gate:
        gate_ref = refs[pos]
        pos += 1
    h_scr, gate_scr = refs[pos], refs[pos + 1]

    j = pl.program_id(1)
    tm, tn = o_ref.shape

    def store(acc, rows, normed):
        if normed:
            for c in range(tn // group):
                sl = slice(c * group, (c + 1) * group)
                o_ref[rows, sl] = _rms_rows(acc[:, sl], eg_ref[:, sl]).astype(o_ref.dtype)
        else:
            o_ref[rows, :] = acc.astype(o_ref.dtype)

    @pl.when(j == 0)
    def _():
        part = tm // NORM_ROW_PARTS
        for r in range(NORM_ROW_PARTS):
            rows = slice(r * part, (r + 1) * part)
            h = _rms_rows(x_ref[rows, :], g_ref[...]).astype(BF16)
            h_scr[rows, :] = h
            if has_gate:
                gate_scr[rows, :] = jnp.dot(h, wg_ref[...], preferred_element_type=F32)
                gate_ref[:, rows] = gate_scr[rows, :].T
            store(jnp.dot(h, w_ref[...], preferred_element_type=F32), rows, n_norm_tiles > 0)

    @pl.when(j > 0)
    def _():
        acc = jnp.dot(h_scr[...], w_ref[...], preferred_element_type=F32)
        if not group:
            store(acc, slice(None), False)
            return

        @pl.when(j < n_norm_tiles)
        def _():
            store(acc, slice(None), True)

        @pl.when(j >= n_norm_tiles)
        def _():
            store(acc, slice(None), False)


def _norm_matmul(x, g, w, *, w_gate=None, group=0, group_gain=None, n_norm_cols=0):
    t, d = x.shape
    n = w.shape[1]
    tm = min(1024, t)
    tn = 2048 if (n % 2048 == 0 and n_norm_cols % 2048 == 0) else 1024
    has_gate = w_gate is not None
    n_norm_tiles = n_norm_cols // tn if group else 0
    in_specs = [
        pl.BlockSpec((tm, d), lambda i, j: (i, 0)),
        pl.BlockSpec((1, d), lambda i, j: (0, 0)),
        pl.BlockSpec((d, tn), lambda i, j: (0, j)),
    ]
    args = [x, g.reshape(1, d), w]
    if has_gate:
        in_specs.append(pl.BlockSpec((d, LANES), lambda i, j: (0, 0)))
        args.append(w_gate)
    if group:
        last = n_norm_tiles - 1
        in_specs.append(pl.BlockSpec((1, tn), lambda i, j: (0, jnp.minimum(j, last))))
        args.append(group_gain.reshape(1, n_norm_cols))
    out_shape = [jax.ShapeDtypeStruct((t, n), BF16)]
    out_specs = [pl.BlockSpec((tm, tn), lambda i, j: (i, j))]
    if has_gate:
        out_shape.append(jax.ShapeDtypeStruct((LANES, t), F32))
        out_specs.append(pl.BlockSpec((LANES, tm), lambda i, j: (0, i)))
    pipelined = (_nbytes((tm, d), F32) + _nbytes((d, tn), BF16) + _nbytes((tm, tn), BF16)
                 + _nbytes((tm, LANES), F32) + _nbytes((d, LANES), BF16))
    resident = _nbytes((tm, d), BF16) + 3 * _nbytes((tm, tn), F32) + _nbytes((tm, d), F32)
    outs = pl.pallas_call(
        functools.partial(_norm_matmul_kernel, group=group, n_norm_tiles=n_norm_tiles, has_gate=has_gate),
        grid=(t // tm, n // tn),
        in_specs=in_specs,
        out_specs=out_specs,
        out_shape=out_shape,
        scratch_shapes=[pltpu.VMEM((tm, d), BF16), pltpu.VMEM((tm, LANES), F32)],
        compiler_params=pltpu.CompilerParams(
            dimension_semantics=("parallel", "arbitrary"),
            vmem_limit_bytes=_vmem_limit(pipelined, resident)),
        name="norm_matmul",
    )(*args)
    return outs if has_gate else outs[0]


def _out_proj_kernel(al_ref, ar_ref, w_ref, x_ref, o_ref):
    kl = al_ref.shape[1]
    acc = jnp.dot(al_ref[...], w_ref[:kl, :], preferred_element_type=F32)
    acc += jnp.dot(ar_ref[...], w_ref[kl:, :], preferred_element_type=F32)
    o_ref[...] = x_ref[...] + acc


def _out_proj(a_left, col_left, a_right, col_right, w, x):
    t, d = x.shape
    k = w.shape[0]
    kh = k // 2
    tm = min(512, t)
    pipelined = 2 * _nbytes((tm, kh), BF16) + 2 * _nbytes((tm, d), F32) + _nbytes((k, d), BF16)
    return pl.pallas_call(
        _out_proj_kernel,
        grid=(t // tm,),
        in_specs=[
            pl.BlockSpec((tm, kh), lambda i: (i, col_left)),
            pl.BlockSpec((tm, kh), lambda i: (i, col_right)),
            pl.BlockSpec((k, d), lambda i: (0, 0)),
            pl.BlockSpec((tm, d), lambda i: (i, 0)),
        ],
        out_specs=pl.BlockSpec((tm, d), lambda i: (i, 0)),
        out_shape=jax.ShapeDtypeStruct((t, d), F32),
        compiler_params=pltpu.CompilerParams(
            dimension_semantics=("parallel",),
            vmem_limit_bytes=_vmem_limit(pipelined, _nbytes((tm, d), F32))),
        name="out_proj",
    )(a_left, a_right, w, x)


def _mlp_kernel(x_ref, g_ref, w1_ref, w2_ref, o_ref, h_scr):
    def hidden_out(h):
        u = jnp.maximum(jnp.dot(h, w1_ref[...], preferred_element_type=F32), 0.0)
        return jnp.dot((u * u).astype(BF16), w2_ref[...], preferred_element_type=F32)

    @pl.when(pl.program_id(1) == 0)
    def _():
        part = x_ref.shape[0] // NORM_ROW_PARTS
        for r in range(NORM_ROW_PARTS):
            rows = slice(r * part, (r + 1) * part)
            x = x_ref[rows, :]
            h = _rms_rows(x, g_ref[...]).astype(BF16)
            h_scr[rows, :] = h
            o_ref[rows, :] = x + hidden_out(h)

    @pl.when(pl.program_id(1) > 0)
    def _():
        o_ref[...] += hidden_out(h_scr[...])


def _mlp(x, g, w1, w2):
    t, d = x.shape
    f = w1.shape[1]
    tm = min(1024, t)
    tf = min(1024, f)
    pipelined = 2 * _nbytes((tm, d), F32) + 2 * _nbytes((d, tf), BF16)
    resident = _nbytes((tm, d), BF16) + 2 * _nbytes((tm, tf), F32)
    return pl.pallas_call(
        _mlp_kernel,
        grid=(t // tm, f // tf),
        in_specs=[
            pl.BlockSpec((tm, d), lambda i, j: (i, 0)),
            pl.BlockSpec((1, d), lambda i, j: (0, 0)),
            pl.BlockSpec((d, tf), lambda i, j: (0, j)),
            pl.BlockSpec((tf, d), lambda i, j: (j, 0)),
        ],
        out_specs=pl.BlockSpec((tm, d), lambda i, j: (i, 0)),
        out_shape=jax.ShapeDtypeStruct((t, d), F32),
        scratch_shapes=[pltpu.VMEM((tm, d), BF16)],
        compiler_params=pltpu.CompilerParams(
            dimension_semantics=("parallel", "arbitrary"),
            vmem_limit_bytes=_vmem_limit(pipelined, resident)),
        name="mlp",
    )(x, g.reshape(1, d), w1, w2)


def _upper_tri_ones():
    r = lax.broadcasted_iota(jnp.int32, (LANES, LANES), 0)
    c = lax.broadcasted_iota(jnp.int32, (LANES, LANES), 1)
    return jnp.where(r <= c, 1.0, 0.0).astype(BF16)


def _cumsum_lanes(x, tri):
    hi = x.astype(BF16)
    r1 = x - hi.astype(F32)
    mid = r1.astype(BF16)
    lo = (r1 - mid.astype(F32)).astype(BF16)
    out = jnp.dot(hi, tri, preferred_element_type=F32)
    out += jnp.dot(mid, tri, preferred_element_type=F32)
    out += jnp.dot(lo, tri, preferred_element_type=F32)
    return out


def _fox_gate_kernel(f_ref, b_ref, o_ref, carry_scr):
    @pl.when(pl.program_id(1) == 0)
    def _():
        carry_scr[...] = jnp.zeros_like(carry_scr)

    tri = _upper_tri_ones()
    carry = carry_scr[:, 0:1]
    for c in range(f_ref.shape[1] // LANES):
        sl = slice(c * LANES, (c + 1) * LANES)
        ls = _log_sigmoid(f_ref[:, sl] + b_ref[...])
        cum = _cumsum_lanes(ls, tri) + carry
        carry = cum[:, LANES - 1:LANES]
        c2 = cum * LOG2E
        hi = c2.astype(BF16).astype(F32)
        mid = (c2 - hi).astype(BF16).astype(F32)
        o_ref[0, 0, :, sl] = hi
        o_ref[1, 0, :, sl] = mid
        o_ref[2, 0, :, sl] = ((c2 - hi) - mid).astype(BF16).astype(F32)
    carry_scr[...] = jnp.broadcast_to(carry, carry_scr.shape)


def _fox_gates(f_rows, b_f, bsz, s):
    h = f_rows.shape[0]
    lb = min(2048, s)
    nblk = s // lb
    return pl.pallas_call(
        _fox_gate_kernel,
        grid=(bsz, nblk),
        in_specs=[pl.BlockSpec((h, lb), lambda b, i: (0, b * nblk + i)),
                  pl.BlockSpec((h, 1), lambda b, i: (0, 0))],
        out_specs=pl.BlockSpec((3, 1, h, lb), lambda b, i: (0, b, 0, i)),
        out_shape=jax.ShapeDtypeStruct((3, bsz, h, s), F32),
        scratch_shapes=[pltpu.VMEM((h, LANES), F32)],
        compiler_params=pltpu.CompilerParams(dimension_semantics=("parallel", "arbitrary")),
        name="fox_gates",
    )(f_rows, b_f.reshape(h, 1))


def _prefix_max_lanes(x):
    lane = lax.broadcasted_iota(jnp.int32, x.shape, 1)
    shift = 1
    while shift < LANES:
        x = jnp.maximum(x, jnp.where(lane >= shift, pltpu.roll(x, shift, 1), -jnp.inf))
        shift *= 2
    return x


def _mlstm_gate_kernel(i_ref, f_ref, bi_ref, bf_ref, ao_ref, bo_ref, mo_ref):
    tri = _upper_tri_ones()
    for c in range(f_ref.shape[1] // LANES):
        sl = slice(c * LANES, (c + 1) * LANES)
        b = _cumsum_lanes(_log_sigmoid(f_ref[:, sl] + bf_ref[...]), tri)
        a = i_ref[:, sl] + bi_ref[...] - b
        bo_ref[0, :, sl] = b
        ao_ref[0, :, sl] = a
        mo_ref[0, :, sl] = _prefix_max_lanes(a)


def _mlstm_gates(i_rows, f_rows, b_i, b_f, bsz, s):
    assert MLSTM_CHUNK == LANES
    h = f_rows.shape[0]
    lb = min(2048, s)
    nblk = s // lb
    row_in = pl.BlockSpec((h, lb), lambda b, i: (0, b * nblk + i))
    row = pl.BlockSpec((1, h, lb), lambda b, i: (b, 0, i))
    bias = pl.BlockSpec((h, 1), lambda b, i: (0, 0))
    return pl.pallas_call(
        _mlstm_gate_kernel,
        grid=(bsz, nblk),
        in_specs=[row_in, row_in, bias, bias],
        out_specs=[row, row, row],
        out_shape=[jax.ShapeDtypeStruct((bsz, h, s), F32)] * 3,
        compiler_params=pltpu.CompilerParams(dimension_semantics=("parallel", "parallel")),
        name="mlstm_gates",
    )(i_rows, f_rows, b_i.reshape(h, 1), b_f.reshape(h, 1))


FOX_CHAINS = 2
FOX_BIAS_LANES = LANES // N_HEADS
FOX_SKIP_BITS = 150.0


def _fox_kernel(qkb_ref, fq_ref, fk_ref, q_ref, qx_ref, k_ref, kx_ref, vt_ref, o_ref, qt_scr, st_scr, mx_scr,
                *, blk):
    qi = pl.program_id(2)
    bh = pl.program_id(0) * N_HEADS + pl.program_id(1)
    lane = lax.broadcasted_iota(jnp.int32, qx_ref.shape, 1)
    owner = lax.shift_right_logical(lane, FOX_BIAS_LANES.bit_length() - 1)
    qx = jnp.where(owner == pl.program_id(1), qx_ref[...].astype(F32), 0.0)
    q_aug = jnp.concatenate([q_ref[...].astype(F32), qx], axis=1)
    qt_scr[...] = q_aug.T.astype(BF16)
    chains = range(FOX_CHAINS)
    ones_rows = jnp.ones((BF16_SUBLANES, blk), BF16)

    def col_max(st):
        part = jnp.max(st.reshape(blk // 16, 16, blk), axis=0)
        return jnp.max(part, axis=0, keepdims=True)

    def scores_into(slot, j, which):
        off = pl.multiple_of(j * blk, blk)
        k_aug = jnp.concatenate([k_ref[pl.ds(off, blk), :], kx_ref[pl.ds(off, blk), :]], axis=1)
        for u in which:
            st = jnp.dot(k_aug, qt_scr[:, u * blk:(u + 1) * blk], preferred_element_type=F32)
            st_scr[slot, u] = st
            mx_scr[slot, u] = col_max(st)

    def consume(slot, j, carry, which, diagonal_chain=None):
        vt = jnp.concatenate([vt_ref[0, 0, j], ones_rows], axis=0)
        carry = list(carry)
        for u in which:
            m, acc = carry[u]
            st = st_scr[slot, u]
            if u == diagonal_chain:
                key = lax.broadcasted_iota(jnp.int32, st.shape, 0)
                qry = lax.broadcasted_iota(jnp.int32, st.shape, 1)
                st = jnp.where(key <= qry, st, -1e30)
                mx = col_max(st)
            else:
                mx = mx_scr[slot, u]
            m_new = jnp.maximum(m, mx)
            p = jnp.exp2(st - m_new).astype(BF16)
            acc = jnp.exp2(m - m_new) * acc + jnp.dot(vt, p, preferred_element_type=F32)
            carry[u] = (m_new, acc)
        return tuple(carry)

    f_first_query = fq_ref[bh, qi]
    qkb = qkb_ref[0]

    def needed(a, carry):
        lowest_max = jnp.min(jnp.minimum(carry[0][0], carry[1][0]))
        bound = qkb + f_first_query - fk_ref[bh, jnp.maximum(a, 0)]
        return jnp.logical_and(a >= 1, bound > lowest_max - FOX_SKIP_BITS).astype(jnp.int32)

    def pair(state):
        a, _, carry = state
        scores_into(0, a - 1, chains)
        carry = consume(1, a, carry, chains)
        scores_into(1, jnp.maximum(a - 2, 0), chains)
        carry = consume(0, a - 1, carry, chains)
        return a - 2, needed(a - 2, carry), carry

    init = tuple((jnp.full((1, blk), -1e30, F32), jnp.zeros((HEAD_DIM + BF16_SUBLANES, blk), F32))
                 for _ in chains)
    scores_into(1, 2 * qi + 1, [1])
    scores_into(0, 2 * qi, chains)
    carry = consume(1, 2 * qi + 1, init, [1], diagonal_chain=1)
    older = 2 * qi - 1
    scores_into(1, jnp.maximum(older, 0), chains)
    carry = consume(0, 2 * qi, carry, chains, diagonal_chain=0)
    _, _, carry = lax.while_loop(lambda state: state[1] > 0, pair, (older, needed(older, carry), carry))
    for u in chains:
        acc = carry[u][1]
        out = acc[:HEAD_DIM, :] / acc[HEAD_DIM:HEAD_DIM + 1, :]
        o_ref[u * blk:(u + 1) * blk, :] = out.T.astype(o_ref.dtype)


def _fox_bias_lanes(pieces, bsz, s):
    p = pieces.astype(BF16).transpose(1, 3, 2, 0)
    ones = jnp.ones_like(p)
    pad = jnp.zeros((bsz, s, N_HEADS, FOX_BIAS_LANES - 6), BF16)
    qx = jnp.concatenate([p, ones, pad], axis=-1).reshape(bsz * s, LANES)
    kx = jnp.concatenate([ones, -p, pad], axis=-1).reshape(bsz * s, LANES)
    return qx, kx


def _fox_attention(proj, pieces, qk_bound, bsz, s):
    t = proj.shape[0]
    blk = 512
    tq = FOX_CHAINS * blk
    assert s % tq == 0
    nb, nq = s // blk, s // tq
    fox_w = N_HEADS * HEAD_DIM
    qx, kx = _fox_bias_lanes(pieces, bsz, s)
    f2 = (pieces[0] + pieces[1] + pieces[2]).reshape(bsz * N_HEADS, s)
    f_first_query = f2[:, ::tq]
    f_last_key = f2[:, blk - 1::blk]
    vt = proj[:, 2 * fox_w:3 * fox_w].reshape(bsz, nb, blk, N_HEADS, HEAD_DIM).transpose(0, 3, 1, 4, 2)
    pipelined = 3 * _nbytes((s, HEAD_DIM), BF16) + 3 * _nbytes((tq, HEAD_DIM), BF16)
    resident = (2 + 4) * FOX_CHAINS * _nbytes((blk, blk), F32)
    smem = pl.BlockSpec(memory_space=pltpu.SMEM)
    return pl.pallas_call(
        functools.partial(_fox_kernel, blk=blk),
        grid=(bsz, N_HEADS, nq),
        in_specs=[
            smem, smem, smem,
            pl.BlockSpec((tq, HEAD_DIM), lambda b, h, i: (b * nq + i, h)),
            pl.BlockSpec((tq, LANES), lambda b, h, i: (b * nq + i, 0)),
            pl.BlockSpec((s, HEAD_DIM), lambda b, h, i: (b, N_HEADS + h)),
            pl.BlockSpec((s, LANES), lambda b, h, i: (b, 0)),
            pl.BlockSpec((1, 1, nb, HEAD_DIM, blk), lambda b, h, i: (b, h, 0, 0, 0)),
        ],
        out_specs=pl.BlockSpec((tq, HEAD_DIM), lambda b, h, i: (b * nq + i, h)),
        out_shape=jax.ShapeDtypeStruct((t, fox_w), BF16),
        scratch_shapes=[pltpu.VMEM((2 * HEAD_DIM, tq), BF16),
                        pltpu.VMEM((2, FOX_CHAINS, blk, blk), F32),
                        pltpu.VMEM((2, FOX_CHAINS, 1, blk), F32)],
        compiler_params=pltpu.CompilerParams(
            dimension_semantics=("parallel", "parallel", "arbitrary"),
            vmem_limit_bytes=_vmem_limit(pipelined, resident)),
        name="fox_attention",
    )(qk_bound.reshape(1).astype(F32), f_first_query, f_last_key, proj, qx, proj, kx, vt)


def _retention_tables(chunk):
    h = np.arange(N_HEADS, dtype=np.float64)
    log_g = np.log1p(-np.exp2(-5.0 - h))
    pos = np.arange(chunk, dtype=np.float64)
    rel = pos[:, None] - pos[None, :]
    intra = np.where(rel >= 0, np.exp(log_g[:, None, None] * np.maximum(rel, 0.0)), 0.0)
    q_dec = np.exp(log_g[:, None] * (pos + 1.0))[..., None] * np.ones((1, 1, LANES))
    k_dec = np.exp(log_g[:, None] * (chunk - 1.0 - pos))[..., None] * np.ones((1, 1, LANES))
    c_dec = np.exp(log_g * chunk)[:, None, None] * np.ones((1, 1, LANES))
    return tuple(jnp.asarray(a, F32) for a in (intra, q_dec, k_dec, c_dec))


def _rope(t, cos, sin):
    return t * cos + pltpu.roll(t, HEAD_DIM // 2, 1) * sin


def _retention_kernel(q_ref, k_ref, v_ref, g_ref, cos_ref, sin_ref, intra_ref, qd_ref, kd_ref, cd_ref,
                      gr_ref, o_ref, state_scr, *, chunk):
    @pl.when(pl.program_id(2) == 0)
    def _():
        state_scr[...] = jnp.zeros_like(state_scr)

    intra, q_dec, k_dec, c_dec = intra_ref[0], qd_ref[0], kd_ref[0], cd_ref[0]
    state = state_scr[...]
    for c in range(q_ref.shape[0] // chunk):
        rows = slice(c * chunk, (c + 1) * chunk)
        cos, sin = cos_ref[rows, :], sin_ref[rows, :]
        q = _rope(q_ref[rows, :].astype(F32), cos, sin)
        k = _rope(k_ref[rows, :].astype(F32), cos, sin) * (HEAD_DIM ** -0.5)
        v = v_ref[rows, :]
        sc = lax.dot_general(q.astype(BF16), k.astype(BF16), (((1,), (1,)), ((), ())),
                             preferred_element_type=F32) * intra
        o = jnp.dot(sc.astype(BF16), v, preferred_element_type=F32)
        o += jnp.dot((q * q_dec).astype(BF16), state.astype(BF16), preferred_element_type=F32)
        state = c_dec * state + lax.dot_general((k * k_dec).astype(BF16), v, (((0,), (0,)), ((), ())),
                                                preferred_element_type=F32)
        y = o - jnp.mean(o, axis=-1, keepdims=True)
        y = y * lax.rsqrt(jnp.mean(y * y, axis=-1, keepdims=True) + EPS)
        gate = g_ref[rows, :].astype(F32)
        o_ref[rows, :] = (y * gr_ref[...] * (gate * _sigmoid(gate))).astype(o_ref.dtype)
    state_scr[...] = state


def _retention(proj, col0, g_ret, cos_t, sin_t, bsz, s):
    t = proj.shape[0]
    chunk = min(RET_CHUNK, s)
    lb = min(1024, s)
    nb = s // lb
    intra, q_dec, k_dec, c_dec = _retention_tables(chunk)

    def col(group):
        return pl.BlockSpec((lb, HEAD_DIM), lambda b, h, i: (b * nb + i, col0 + group * N_HEADS + h))

    def table(shape):
        return pl.BlockSpec((1,) + shape, lambda b, h, i: (h, 0, 0))

    pipelined = 5 * _nbytes((lb, HEAD_DIM), BF16) + 2 * _nbytes((lb, HEAD_DIM), F32)
    resident = 2 * (_nbytes((chunk, chunk), F32) + 2 * _nbytes((chunk, LANES), F32)) + 8 * _nbytes((chunk, chunk), F32)
    return pl.pallas_call(
        functools.partial(_retention_kernel, chunk=chunk),
        grid=(bsz, N_HEADS, nb),
        in_specs=[
            col(0), col(1), col(2), col(3),
            pl.BlockSpec((lb, HEAD_DIM), lambda b, h, i: (i, 0)),
            pl.BlockSpec((lb, HEAD_DIM), lambda b, h, i: (i, 0)),
            table((chunk, chunk)), table((chunk, LANES)), table((chunk, LANES)), table((1, LANES)),
            pl.BlockSpec((1, HEAD_DIM), lambda b, h, i: (0, h)),
        ],
        out_specs=pl.BlockSpec((lb, HEAD_DIM), lambda b, h, i: (b * nb + i, h)),
        out_shape=jax.ShapeDtypeStruct((t, N_HEADS * HEAD_DIM), BF16),
        scratch_shapes=[pltpu.VMEM((HEAD_DIM, HEAD_DIM), F32)],
        compiler_params=pltpu.CompilerParams(
            dimension_semantics=("parallel", "parallel", "arbitrary"),
            vmem_limit_bytes=_vmem_limit(pipelined, resident)),
        name="retention",
    )(proj, proj, proj, proj, cos_t, sin_t, intra, q_dec, k_dec, c_dec, g_ret.reshape(1, -1))


def _conv_silu(cur, tail, w):
    row8 = lax.broadcasted_iota(jnp.int32, (8, cur.shape[1]), 0)
    out = cur * w[CONV_W - 1:CONV_W, :]
    for shift in range(1, CONV_W):
        rolled = pltpu.roll(cur, shift, 0)
        head = jnp.where(row8 < shift, pltpu.roll(tail, shift, 0), rolled[:8, :])
        shifted = jnp.concatenate([head, rolled[8:, :]], axis=0)
        out += shifted * w[CONV_W - 1 - shift:CONV_W - shift, :]
    return out * _sigmoid(out)


def _split3(x):
    hi = x.astype(BF16).astype(F32)
    rest = x - hi
    mid = rest.astype(BF16).astype(F32)
    return hi, mid, (rest - mid).astype(BF16).astype(F32)


def _rows16(assign):
    sub = lax.broadcasted_iota(jnp.int32, (BF16_SUBLANES, LANES), 0)
    out = jnp.zeros((BF16_SUBLANES, LANES), F32)
    for r, val in assign.items():
        out = jnp.where(sub == r, val, out)
    return out


def _mlstm_kernel(q_ref, k_ref, v_ref, og_ref, wq_ref, wk_ref, a_ref, b_ref, am_ref, gh_ref, o_ref,
                  c_scr, m_scr, tail_scr, *, chunk):
    @pl.when(pl.program_id(2) == 0)
    def _():
        c_scr[...] = jnp.zeros_like(c_scr)
        m_scr[...] = jnp.zeros_like(m_scr)
        tail_scr[...] = jnp.zeros_like(tail_scr)

    dv = MLSTM_V
    causal = (lax.broadcasted_iota(jnp.int32, (chunk, chunk), 1)
              <= lax.broadcasted_iota(jnp.int32, (chunk, chunk), 0))
    ones_cols = jnp.ones((chunk, LANES), BF16)
    mean_cols = jnp.ones((dv, LANES), BF16)
    rep_g = _rows16({0: 1.0, 1: 1.0, 2: 1.0})
    rep_b = _rows16({3: 1.0, 4: 1.0, 5: 1.0})
    rep_a = _rows16({6: 1.0, 7: 1.0, 8: 1.0})

    chunks = range(q_ref.shape[0] // chunk)
    halves = (slice(0, LANES), slice(LANES, dv))

    def rows(c):
        return slice(c * chunk, (c + 1) * chunk)

    tail = tail_scr[...]
    w_qk = jnp.concatenate([wq_ref[...], wk_ref[...]], axis=1)
    q_b, k_f, sc_raw, v_aug, b_row, a_row, a_cummax, b_last, lw_max = ([] for _ in range(9))
    for c in chunks:
        cur = jnp.concatenate([q_ref[rows(c), :], k_ref[rows(c), :]], axis=1).astype(F32)
        qk = _conv_silu(cur, tail, w_qk)
        tail = cur[chunk - 8:, :]
        q_b.append(qk[:, :MLSTM_QK].astype(BF16))
        k_f.append(qk[:, MLSTM_QK:] * (MLSTM_QK ** -0.5))
        sc_raw.append(lax.dot_general(q_b[c], k_f[c].astype(BF16), (((1,), (1,)), ((), ())),
                                      preferred_element_type=F32))
        v_aug.append(jnp.concatenate([v_ref[rows(c), :], ones_cols], axis=1))
        b_row.append(b_ref[0, :, rows(c)])
        a_row.append(a_ref[0, :, rows(c)])
        a_cummax.append(am_ref[0, :, rows(c)])
        b_last.append(b_row[c][:, chunk - 1:chunk])
        lw_max.append(b_last[c] + a_cummax[c][:, chunk - 1:chunk])
    tail_scr[...] = tail

    m_in, m_out = [], []
    m_st = m_scr[0:1, 0:1]
    for c in chunks:
        m_in.append(m_st)
        m_st = jnp.maximum(b_last[c] + m_st, lw_max[c])
        m_out.append(m_st)
    m_scr[...] = jnp.broadcast_to(m_st, m_scr.shape)

    rep = []
    for c in chunks:
        g3, b3, a3 = _split3(jnp.maximum(a_cummax[c], m_in[c])), _split3(b_row[c]), _split3(a_row[c])
        lhs = _rows16({0: g3[0], 1: g3[1], 2: g3[2], 3: b3[0], 4: b3[1], 5: b3[2],
                       6: a3[0], 7: a3[1], 8: a3[2], 9: 1.0, 10: 1.0, 11: 1.0})
        rep_e = _rows16({0: -1.0, 1: -1.0, 2: -1.0, 9: a3[0], 10: a3[1], 11: a3[2]})
        rhs = jnp.concatenate([rep_e, rep_g, rep_b, rep_a], axis=1)
        rep.append(lax.dot_general(lhs.astype(BF16), rhs.astype(BF16), (((0,), (0,)), ((), ())),
                                   preferred_element_type=F32))

    num, inc = [], []
    for c in chunks:
        e_mat, a_rep = rep[c][:, 0:LANES], rep[c][:, 3 * LANES:]
        sc = jnp.where(causal, sc_raw[c] * jnp.exp(e_mat), 0.0)
        num.append(jnp.dot(sc.astype(BF16), v_aug[c], preferred_element_type=F32))
        kw = k_f[c] * jnp.exp(b_last[c] + a_rep - m_out[c])
        inc.append(lax.dot_general(kw.astype(BF16), v_aug[c], (((0,), (0,)), ((), ())),
                                   preferred_element_type=F32))

    c_in = []
    c_aug = c_scr[...]
    for c in chunks:
        c_in.append(c_aug.astype(BF16))
        c_aug = jnp.exp(b_last[c] + m_in[c] - m_out[c]) * c_aug + inc[c]
    c_scr[...] = c_aug

    hid, sq_sum = [], []
    for c in chunks:
        g_rep, b_rep = rep[c][:, LANES:2 * LANES], rep[c][:, 2 * LANES:3 * LANES]
        w_inter = jnp.exp(m_in[c] - g_rep)
        inter = jnp.dot(q_b[c], c_in[c], preferred_element_type=F32)
        den = num[c][:, dv:] + w_inter * inter[:, dv:]
        inv = 1.0 / jnp.maximum(jnp.abs(den), jnp.exp(-(b_rep + g_rep)))
        hid.append([(num[c][:, sl] + w_inter * inter[:, sl]) * inv for sl in halves])
        sq = jnp.concatenate([hh * hh for hh in hid[c]], axis=1).astype(BF16)
        sq_sum.append(jnp.dot(sq, mean_cols, preferred_element_type=F32))

    for c in chunks:
        scale = lax.rsqrt(sq_sum[c] * (1.0 / dv) + EPS)
        for part, sl in zip(hid[c], halves):
            gate = _sigmoid(og_ref[rows(c), sl].astype(F32))
            o_ref[rows(c), sl] = (part * scale * gh_ref[:, sl] * gate).astype(o_ref.dtype)


def _mlstm(proj, conv_w, gate_rows, g_h, bsz, s):
    t = proj.shape[0]
    chunk = MLSTM_CHUNK
    assert chunk == LANES and MLSTM_QK == LANES
    lb = min(512, s)
    nb = s // lb
    vcol = 2 * N_HEADS * MLSTM_QK // MLSTM_V
    gates = pl.BlockSpec((1, 1, lb), lambda b, h, i: (b * N_HEADS + h, 0, i))
    pipelined = 2 * _nbytes((lb, MLSTM_QK), BF16) + 3 * _nbytes((lb, MLSTM_V), BF16)
    resident = 2 * _nbytes((MLSTM_QK, MLSTM_V + LANES), F32) + 16 * _nbytes((chunk, MLSTM_V + LANES), F32)
    return pl.pallas_call(
        functools.partial(_mlstm_kernel, chunk=chunk),
        grid=(bsz, N_HEADS, nb),
        in_specs=[
            pl.BlockSpec((lb, MLSTM_QK), lambda b, h, i: (b * nb + i, h)),
            pl.BlockSpec((lb, MLSTM_QK), lambda b, h, i: (b * nb + i, N_HEADS + h)),
            pl.BlockSpec((lb, MLSTM_V), lambda b, h, i: (b * nb + i, vcol + h)),
            pl.BlockSpec((lb, MLSTM_V), lambda b, h, i: (b * nb + i, vcol + N_HEADS + h)),
            pl.BlockSpec((CONV_W, MLSTM_QK), lambda b, h, i: (0, h)),
            pl.BlockSpec((CONV_W, MLSTM_QK), lambda b, h, i: (0, N_HEADS + h)),
            gates, gates, gates,
            pl.BlockSpec((1, MLSTM_V), lambda b, h, i: (0, h)),
        ],
        out_specs=pl.BlockSpec((lb, MLSTM_V), lambda b, h, i: (b * nb + i, h)),
        out_shape=jax.ShapeDtypeStruct((t, N_HEADS * MLSTM_V), BF16),
        scratch_shapes=[
            pltpu.VMEM((MLSTM_QK, MLSTM_V + LANES), F32),
            pltpu.VMEM((8, LANES), F32),
            pltpu.VMEM((8, 2 * MLSTM_QK), F32),
        ],
        compiler_params=pltpu.CompilerParams(
            dimension_semantics=("parallel", "parallel", "arbitrary"),
            vmem_limit_bytes=_vmem_limit(pipelined, resident)),
        name="mlstm",
    )(proj, proj, proj, proj, conv_w, conv_w,
      *(r.reshape(bsz * N_HEADS, 1, s) for r in gate_rows), g_h.reshape(1, -1))


XATTN_ROW_PARTS = 2


def _xattn_out_kernel(q_ref, k_ref, v_ref, w_ref, x_ref, o_ref):
    tm, d = q_ref.shape
    dh = d // N_XATTN
    part = tm // XATTN_ROW_PARTS
    heads = [slice(h * dh, (h + 1) * dh) for h in range(N_XATTN)]
    parts = [slice(r * part, (r + 1) * part) for r in range(XATTN_ROW_PARTS)]
    scores = {(r, h): lax.dot_general(q_ref[parts[r], heads[h]], k_ref[:, heads[h]], (((1,), (1,)), ((), ())),
                                      preferred_element_type=F32)
              for r in range(XATTN_ROW_PARTS) for h in range(N_XATTN)}
    for r in range(XATTN_ROW_PARTS):
        outs = []
        for h in range(N_XATTN):
            s = scores[(r, h)]
            p = jnp.exp(s - jnp.max(s, axis=-1, keepdims=True))
            p = p / jnp.sum(p, axis=-1, keepdims=True)
            outs.append(jnp.dot(p.astype(BF16), v_ref[:, heads[h]], preferred_element_type=F32).astype(BF16))
        attn = jnp.concatenate(outs, axis=1)
        o_ref[parts[r], :] = x_ref[parts[r], :] + jnp.dot(attn, w_ref[...], preferred_element_type=F32)


def _xattn_out(q, kv, w, x, s, n_mem):
    t, d = q.shape
    tm = min(512, s)
    per_seq = s // tm
    pipelined = (_nbytes((tm, d), BF16) + 2 * _nbytes((n_mem, d), BF16) + _nbytes((d, d), BF16)
                 + 2 * _nbytes((tm, d), F32))
    resident = (N_XATTN * XATTN_ROW_PARTS + 4) * _nbytes((tm // XATTN_ROW_PARTS, n_mem), F32) + _nbytes((tm, d), F32)
    return pl.pallas_call(
        _xattn_out_kernel,
        grid=(t // tm,),
        in_specs=[
            pl.BlockSpec((tm, d), lambda i: (i, 0)),
            pl.BlockSpec((n_mem, d), lambda i: (i // per_seq, 0)),
            pl.BlockSpec((n_mem, d), lambda i: (i // per_seq, 1)),
            pl.BlockSpec((d, d), lambda i: (0, 0)),
            pl.BlockSpec((tm, d), lambda i: (i, 0)),
        ],
        out_specs=pl.BlockSpec((tm, d), lambda i: (i, 0)),
        out_shape=jax.ShapeDtypeStruct((t, d), F32),
        compiler_params=pltpu.CompilerParams(
            dimension_semantics=("parallel",),
            vmem_limit_bytes=_vmem_limit(pipelined, resident)),
        name="xattn_out",
    )(q, kv, kv, w, x)


def _gate_weight(w_cols):
    return jnp.pad(w_cols, ((0, 0), (0, LANES - w_cols.shape[1]))).astype(BF16)


def _rope_tables(s):
    inv = 1.0 / (ROPE_BASE ** (jnp.arange(0, HEAD_DIM, 2, dtype=F32) / HEAD_DIM))
    ang = jnp.arange(s, dtype=F32)[:, None] * inv[None, :]
    cos, sin = jnp.cos(ang), jnp.sin(ang)
    return jnp.concatenate([cos, cos], axis=-1), jnp.concatenate([-sin, sin], axis=-1)


def _even_mixer(x, g, w_in, b_f, g_q, g_k, g_ret, w_out, rope, bsz, s):
    fox_w = N_HEADS * HEAD_DIM
    w_main = jnp.concatenate([w_in[:, :3 * fox_w], w_in[:, 3 * fox_w + N_HEADS:]], axis=1).astype(BF16)
    w_gate = _gate_weight(w_in[:, 3 * fox_w:3 * fox_w + N_HEADS])
    qk_gain = jnp.concatenate([jnp.tile(g_q * (HEAD_DIM ** -0.5 * LOG2E), N_HEADS), jnp.tile(g_k, N_HEADS)])
    proj, gate = _norm_matmul(x, g, w_main, w_gate=w_gate, group=HEAD_DIM, group_gain=qk_gain,
                              n_norm_cols=2 * fox_w)
    pieces = _fox_gates(gate[:N_HEADS], b_f, bsz, s)
    qk_bound = 1.01 * HEAD_DIM ** 0.5 * LOG2E * jnp.max(jnp.abs(g_q)) * jnp.max(jnp.abs(g_k))
    ya = _fox_attention(proj, pieces, qk_bound, bsz, s)
    yb = _retention(proj, 3 * N_HEADS, g_ret, rope[0], rope[1], bsz, s)
    return _out_proj(ya, 0, yb, 0, w_out.astype(BF16), x)


def _odd_mixer(x, g, w_in, conv_w, b_i, b_f, g_h, w_out, bsz, s):
    n_main = 2 * N_HEADS * MLSTM_QK + 2 * N_HEADS * MLSTM_V
    proj, gate = _norm_matmul(x, g, w_in[:, :n_main].astype(BF16), w_gate=_gate_weight(w_in[:, n_main:]))
    gate_rows = _mlstm_gates(gate[:N_HEADS], gate[N_HEADS:2 * N_HEADS], b_i, b_f, bsz, s)
    y = _mlstm(proj, conv_w, gate_rows, g_h, bsz, s)
    return _out_proj(y, 0, y, 1, w_out.astype(BF16), x)


def _cross_attention(x, mem, g_x, g_m, wq, wk, wv, wo, g_q, g_k, s):
    d = x.shape[1]
    dh = d // N_XATTN
    n_mem = mem.shape[0] * s // x.shape[0]
    kv = _norm_matmul(mem, g_m, jnp.concatenate([wk, wv], axis=1).astype(BF16),
                      group=dh, group_gain=jnp.tile(g_k, N_XATTN), n_norm_cols=d)
    q = _norm_matmul(x, g_x, wq.astype(BF16), group=dh, group_gain=jnp.tile(g_q * dh ** -0.5, N_XATTN),
                     n_norm_cols=d)
    return _xattn_out(q, kv, wo.astype(BF16), x, s, n_mem)


def kernel(x, mem, norm_mix, norm_xattn, norm_mem, norm_ffn, ev_w_in, ev_b_f, ev_g_q, ev_g_k, ev_g_ret, ev_w_out, od_w_in, od_conv, od_b_i, od_b_f, od_g_h, od_w_out, xa_wq, xa_wk, xa_wv, xa_wo, xa_g_q, xa_g_k, ffn_w1, ffn_w2):
    bsz, s, d = x.shape
    depth = norm_mix.shape[0]
    xf = x.reshape(bsz * s, d)
    memf = mem.reshape(-1, d)
    rope = _rope_tables(s)
    for l in range(depth):
        if l % 2 == 0:
            e = l // 2
            xf = _even_mixer(xf, norm_mix[l], ev_w_in[e], ev_b_f[e], ev_g_q[e], ev_g_k[e], ev_g_ret[e],
                             ev_w_out[e], rope, bsz, s)
        else:
            o = l // 2
            xf = _odd_mixer(xf, norm_mix[l], od_w_in[o], od_conv[o], od_b_i[o], od_b_f[o], od_g_h[o],
                            od_w_out[o], bsz, s)
        xf = _cross_attention(xf, memf, norm_xattn[l], norm_mem[l], xa_wq[l], xa_wk[l], xa_wv[l], xa_wo[l],
                              xa_g_q[l], xa_g_k[l], s)
        xf = _mlp(xf, norm_ffn[l], ffn_w1[l].astype(BF16), ffn_w2[l].astype(BF16))
    return xf.reshape(bsz, s, d)
```

```python
import functools

import numpy as np
import jax
import jax.numpy as jnp
from jax import lax
from jax.experimental import pallas as pl
from jax.experimental.pallas import tpu as pltpu

F32 = jnp.float32
BF16 = jnp.bfloat16

EPS = 1e-6
LOG2E = 1.4426950408889634
LANES = 128
BF16_SUBLANES = 16
HEAD_DIM = 128
N_HEADS = 8
MLSTM_QK = 128
MLSTM_V = 256
N_XATTN = 4
CONV_W = 4
ROPE_BASE = 10000.0
MLSTM_CHUNK = 128
RET_CHUNK = 256
V7X_VMEM_BUDGET = 60 * 1024 * 1024
NORM_ROW_PARTS = 4


def _vmem_limit(pipelined_bytes, resident_bytes):
    return int(min(2 * pipelined_bytes + resident_bytes + (4 << 20), V7X_VMEM_BUDGET))


def _nbytes(shape, dtype):
    return int(np.prod(shape)) * jnp.dtype(dtype).itemsize


def _rms_rows(x, g):
    return x * lax.rsqrt(jnp.mean(x * x, axis=-1, keepdims=True) + EPS) * g


def _sigmoid(x):
    return 1.0 / (1.0 + jnp.exp(-x))


def _log_sigmoid(x):
    return jnp.minimum(x, 0.0) - jnp.log1p(jnp.exp(-jnp.abs(x)))


def _norm_matmul_kernel(*refs, group, n_norm_tiles, has_gate):
    x_ref, g_ref, w_ref = refs[:3]
    pos = 3
    wg_ref = eg_ref = gate_ref = None
    if has_gate:
        wg_ref = refs[pos]
        pos += 1
    if group:
        eg_ref = refs[pos]
        pos += 1
    o_ref = refs[pos]
    pos += 1
    if has_gate:
        gate_ref = refs[pos]
        pos += 1
    h_scr, gate_scr = refs[pos], refs[pos + 1]

    j = pl.program_id(1)
    tm, tn = o_ref.shape

    def store(acc, rows, normed):
        if normed:
            for c in range(tn // group):
                sl = slice(c * group, (c + 1) * group)
                o_ref[rows, sl] = _rms_rows(acc[:, sl], eg_ref[:, sl]).astype(o_ref.dtype)
        else:
            o_ref[rows, :] = acc.astype(o_ref.dtype)

    @pl.when(j == 0)
    def _():
        part = tm // NORM_ROW_PARTS
        for r in range(NORM_ROW_PARTS):
            rows = slice(r * part, (r + 1) * part)
            h = _rms_rows(x_ref[rows, :], g_ref[...]).astype(BF16)
            h_scr[rows, :] = h
            if has_gate:
                gate_scr[rows, :] = jnp.dot(h, wg_ref[...], preferred_element_type=F32)
                gate_ref[:, rows] = gate_scr[rows, :].T
            store(jnp.dot(h, w_ref[...], preferred_element_type=F32), rows, n_norm_tiles > 0)

    @pl.when(j > 0)
    def _():
        acc = jnp.dot(h_scr[...], w_ref[...], preferred_element_type=F32)
        if not group:
            store(acc, slice(None), False)
            return

        @pl.when(j < n_norm_tiles)
        def _():
            store(acc, slice(None), True)

        @pl.when(j >= n_norm_tiles)
        def _():
            store(acc, slice(None), False)


def _norm_matmul(x, g, w, *, w_gate=None, group=0, group_gain=None, n_norm_cols=0):
    t, d = x.shape
    n = w.shape[1]
    tm = min(1024, t)
    tn = 2048 if (n % 2048 == 0 and n_norm_cols % 2048 == 0) else 1024
    has_gate = w_gate is not None
    n_norm_tiles = n_norm_cols // tn if group else 0
    in_specs = [
        pl.BlockSpec((tm, d), lambda i, j: (i, 0)),
        pl.BlockSpec((1, d), lambda i, j: (0, 0)),
        pl.BlockSpec((d, tn), lambda i, j: (0, j)),
    ]
    args = [x, g.reshape(1, d), w]
    if has_gate:
        in_specs.append(pl.BlockSpec((d, LANES), lambda i, j: (0, 0)))
        args.append(w_gate)
    if group:
        last = n_norm_tiles - 1
        in_specs.append(pl.BlockSpec((1, tn), lambda i, j: (0, jnp.minimum(j, last))))
        args.append(group_gain.reshape(1, n_norm_cols))
    out_shape = [jax.ShapeDtypeStruct((t, n), BF16)]
    out_specs = [pl.BlockSpec((tm, tn), lambda i, j: (i, j))]
    if has_gate:
        out_shape.append(jax.ShapeDtypeStruct((LANES, t), F32))
        out_specs.append(pl.BlockSpec((LANES, tm), lambda i, j: (0, i)))
    pipelined = (_nbytes((tm, d), F32) + _nbytes((d, tn), BF16) + _nbytes((tm, tn), BF16)
                 + _nbytes((tm, LANES), F32) + _nbytes((d, LANES), BF16))
    resident = _nbytes((tm, d), BF16) + 3 * _nbytes((tm, tn), F32) + _nbytes((tm, d), F32)
    outs = pl.pallas_call(
        functools.partial(_norm_matmul_kernel, group=group, n_norm_tiles=n_norm_tiles, has_gate=has_gate),
        grid=(t // tm, n // tn),
        in_specs=in_specs,
        out_specs=out_specs,
        out_shape=out_shape,
        scratch_shapes=[pltpu.VMEM((tm, d), BF16), pltpu.VMEM((tm, LANES), F32)],
        compiler_params=pltpu.CompilerParams(
            dimension_semantics=("parallel", "arbitrary"),
            vmem_limit_bytes=_vmem_limit(pipelined, resident)),
        name="norm_matmul",
    )(*args)
    return outs if has_gate else outs[0]


def _out_proj_kernel(al_ref, ar_ref, w_ref, x_ref, o_ref):
    kl = al_ref.shape[1]
    acc = jnp.dot(al_ref[...], w_ref[:kl, :], preferred_element_type=F32)
    acc += jnp.dot(ar_ref[...], w_ref[kl:, :], preferred_element_type=F32)
    o_ref[...] = x_ref[...] + acc


def _out_proj(a_left, col_left, a_right, col_right, w, x):
    t, d = x.shape
    k = w.shape[0]
    kh = k // 2
    tm = min(512, t)
    pipelined = 2 * _nbytes((tm, kh), BF16) + 2 * _nbytes((tm, d), F32) + _nbytes((k, d), BF16)
    return pl.pallas_call(
        _out_proj_kernel,
        grid=(t // tm,),
        in_specs=[
            pl.BlockSpec((tm, kh), lambda i: (i, col_left)),
            pl.BlockSpec((tm, kh), lambda i: (i, col_right)),
            pl.BlockSpec((k, d), lambda i: (0, 0)),
            pl.BlockSpec((tm, d), lambda i: (i, 0)),
        ],
        out_specs=pl.BlockSpec((tm, d), lambda i: (i, 0)),
        out_shape=jax.ShapeDtypeStruct((t, d), F32),
        compiler_params=pltpu.CompilerParams(
            dimension_semantics=("parallel",),
            vmem_limit_bytes=_vmem_limit(pipelined, _nbytes((tm, d), F32))),
        name="out_proj",
    )(a_left, a_right, w, x)


def _mlp_kernel(x_ref, g_ref, w1_ref, w2_ref, o_ref, h_scr):
    def hidden_out(h):
        u = jnp.maximum(jnp.dot(h, w1_ref[...], preferred_element_type=F32), 0.0)
        return jnp.dot((u * u).astype(BF16), w2_ref[...], preferred_element_type=F32)

    @pl.when(pl.program_id(1) == 0)
    def _():
        part = x_ref.shape[0] // NORM_ROW_PARTS
        for r in range(NORM_ROW_PARTS):
            rows = slice(r * part, (r + 1) * part)
            x = x_ref[rows, :]
            h = _rms_rows(x, g_ref[...]).astype(BF16)
            h_scr[rows, :] = h
            o_ref[rows, :] = x + hidden_out(h)

    @pl.when(pl.program_id(1) > 0)
    def _():
        o_ref[...] += hidden_out(h_scr[...])


def _mlp(x, g, w1, w2):
    t, d = x.shape
    f = w1.shape[1]
    tm = min(1024, t)
    tf = min(1024, f)
    pipelined = 2 * _nbytes((tm, d), F32) + 2 * _nbytes((d, tf), BF16)
    resident = _nbytes((tm, d), BF16) + 2 * _nbytes((tm, tf), F32)
    return pl.pallas_call(
        _mlp_kernel,
        grid=(t // tm, f // tf),
        in_specs=[
            pl.BlockSpec((tm, d), lambda i, j: (i, 0)),
            pl.BlockSpec((1, d), lambda i, j: (0, 0)),
            pl.BlockSpec((d, tf), lambda i, j: (0, j)),
            pl.BlockSpec((tf, d), lambda i, j: (j, 0)),
        ],
        out_specs=pl.BlockSpec((tm, d), lambda i, j: (i, 0)),
        out_shape=jax.ShapeDtypeStruct((t, d), F32),
        scratch_shapes=[pltpu.VMEM((tm, d), BF16)],
        compiler_params=pltpu.CompilerParams(
            dimension_semantics=("parallel", "arbitrary"),
            vmem_limit_bytes=_vmem_limit(pipelined, resident)),
        name="mlp",
    )(x, g.reshape(1, d), w1, w2)


def _upper_tri_ones():
    r = lax.broadcasted_iota(jnp.int32, (LANES, LANES), 0)
    c = lax.broadcasted_iota(jnp.int32, (LANES, LANES), 1)
    return jnp.where(r <= c, 1.0, 0.0).astype(BF16)


def _cumsum_lanes(x, tri):
    hi = x.astype(BF16)
    r1 = x - hi.astype(F32)
    mid = r1.astype(BF16)
    lo = (r1 - mid.astype(F32)).astype(BF16)
    out = jnp.dot(hi, tri, preferred_element_type=F32)
    out += jnp.dot(mid, tri, preferred_element_type=F32)
    out += jnp.dot(lo, tri, preferred_element_type=F32)
    return out


def _fox_gate_kernel(f_ref, b_ref, o_ref, carry_scr):
    @pl.when(pl.program_id(1) == 0)
    def _():
        carry_scr[...] = jnp.zeros_like(carry_scr)

    tri = _upper_tri_ones()
    carry = carry_scr[:, 0:1]
    for c in range(f_ref.shape[1] // LANES):
        sl = slice(c * LANES, (c + 1) * LANES)
        ls = _log_sigmoid(f_ref[:, sl] + b_ref[...])
        cum = _cumsum_lanes(ls, tri) + carry
        carry = cum[:, LANES - 1:LANES]
        c2 = cum * LOG2E
        hi = c2.astype(BF16).astype(F32)
        mid = (c2 - hi).astype(BF16).astype(F32)
        o_ref[0, 0, :, sl] = hi
        o_ref[1, 0, :, sl] = mid
        o_ref[2, 0, :, sl] = ((c2 - hi) - mid).astype(BF16).astype(F32)
    carry_scr[...] = jnp.broadcast_to(carry, carry_scr.shape)


def _fox_gates(f_rows, b_f, bsz, s):
    h = f_rows.shape[0]
    lb = min(2048, s)
    nblk = s // lb
    return pl.pallas_call(
        _fox_gate_kernel,
        grid=(bsz, nblk),
        in_specs=[pl.BlockSpec((h, lb), lambda b, i: (0, b * nblk + i)),
                  pl.BlockSpec((h, 1), lambda b, i: (0, 0))],
        out_specs=pl.BlockSpec((3, 1, h, lb), lambda b, i: (0, b, 0, i)),
        out_shape=jax.ShapeDtypeStruct((3, bsz, h, s), F32),
        scratch_shapes=[pltpu.VMEM((h, LANES), F32)],
        compiler_params=pltpu.CompilerParams(dimension_semantics=("parallel", "arbitrary")),
        name="fox_gates",
    )(f_rows, b_f.reshape(h, 1))


def _prefix_max_lanes(x):
    lane = lax.broadcasted_iota(jnp.int32, x.shape, 1)
    shift = 1
    while shift < LANES:
        x = jnp.maximum(x, jnp.where(lane >= shift, pltpu.roll(x, shift, 1), -jnp.inf))
        shift *= 2
    return x


def _mlstm_gate_kernel(i_ref, f_ref, bi_ref, bf_ref, ao_ref, bo_ref, mo_ref):
    tri = _upper_tri_ones()
    for c in range(f_ref.shape[1] // LANES):
        sl = slice(c * LANES, (c + 1) * LANES)
        b = _cumsum_lanes(_log_sigmoid(f_ref[:, sl] + bf_ref[...]), tri)
        a = i_ref[:, sl] + bi_ref[...] - b
        bo_ref[0, :, sl] = b
        ao_ref[0, :, sl] = a
        mo_ref[0, :, sl] = _prefix_max_lanes(a)


def _mlstm_gates(i_rows, f_rows, b_i, b_f, bsz, s):
    assert MLSTM_CHUNK == LANES
    h = f_rows.shape[0]
    lb = min(2048, s)
    nblk = s // lb
    row_in = pl.BlockSpec((h, lb), lambda b, i: (0, b * nblk + i))
    row = pl.BlockSpec((1, h, lb), lambda b, i: (b, 0, i))
    bias = pl.BlockSpec((h, 1), lambda b, i: (0, 0))
    return pl.pallas_call(
        _mlstm_gate_kernel,
        grid=(bsz, nblk),
        in_specs=[row_in, row_in, bias, bias],
        out_specs=[row, row, row],
        out_shape=[jax.ShapeDtypeStruct((bsz, h, s), F32)] * 3,
        compiler_params=pltpu.CompilerParams(dimension_semantics=("parallel", "parallel")),
        name="mlstm_gates",
    )(i_rows, f_rows, b_i.reshape(h, 1), b_f.reshape(h, 1))


FOX_CHAINS = 2
FOX_BIAS_LANES = LANES // N_HEADS
FOX_SKIP_BITS = 150.0


def _fox_kernel(qkb_ref, fq_ref, fk_ref, q_ref, qx_ref, k_ref, kx_ref, vt_ref, o_ref, qt_scr, st_scr, mx_scr,
                *, blk):
    qi = pl.program_id(2)
    bh = pl.program_id(0) * N_HEADS + pl.program_id(1)
    lane = lax.broadcasted_iota(jnp.int32, qx_ref.shape, 1)
    owner = lax.shift_right_logical(lane, FOX_BIAS_LANES.bit_length() - 1)
    qx = jnp.where(owner == pl.program_id(1), qx_ref[...].astype(F32), 0.0)
    q_aug = jnp.concatenate([q_ref[...].astype(F32), qx], axis=1)
    qt_scr[...] = q_aug.T.astype(BF16)
    chains = range(FOX_CHAINS)
    ones_rows = jnp.ones((BF16_SUBLANES, blk), BF16)

    def col_max(st):
        part = jnp.max(st.reshape(blk // 16, 16, blk), axis=0)
        return jnp.max(part, axis=0, keepdims=True)

    def scores_into(slot, j, which):
        off = pl.multiple_of(j * blk, blk)
        k_aug = jnp.concatenate([k_ref[pl.ds(off, blk), :], kx_ref[pl.ds(off, blk), :]], axis=1)
        for u in which:
            st = jnp.dot(k_aug, qt_scr[:, u * blk:(u + 1) * blk], preferred_element_type=F32)
            st_scr[slot, u] = st
            mx_scr[slot, u] = col_max(st)

    def consume(slot, j, carry, which, diagonal_chain=None):
        vt = jnp.concatenate([vt_ref[0, 0, j], ones_rows], axis=0)
        carry = list(carry)
        for u in which:
            m, acc = carry[u]
            st = st_scr[slot, u]
            if u == diagonal_chain:
                key = lax.broadcasted_iota(jnp.int32, st.shape, 0)
                qry = lax.broadcasted_iota(jnp.int32, st.shape, 1)
                st = jnp.where(key <= qry, st, -1e30)
                mx = col_max(st)
            else:
                mx = mx_scr[slot, u]
            m_new = jnp.maximum(m, mx)
            p = jnp.exp2(st - m_new).astype(BF16)
            acc = jnp.exp2(m - m_new) * acc + jnp.dot(vt, p, preferred_element_type=F32)
            carry[u] = (m_new, acc)
        return tuple(carry)

    f_first_query = fq_ref[bh, qi]
    qkb = qkb_ref[0]

    def needed(a, carry):
        lowest_max = jnp.min(jnp.minimum(carry[0][0], carry[1][0]))
        bound = qkb + f_first_query - fk_ref[bh, jnp.maximum(a, 0)]
        return jnp.logical_and(a >= 1, bound > lowest_max - FOX_SKIP_BITS).astype(jnp.int32)

    def pair(state):
        a, _, carry = state
        scores_into(0, a - 1, chains)
        carry = consume(1, a, carry, chains)
        scores_into(1, jnp.maximum(a - 2, 0), chains)
        carry = consume(0, a - 1, carry, chains)
        return a - 2, needed(a - 2, carry), carry

    init = tuple((jnp.full((1, blk), -1e30, F32), jnp.zeros((HEAD_DIM + BF16_SUBLANES, blk), F32))
                 for _ in chains)
    scores_into(1, 2 * qi + 1, [1])
    scores_into(0, 2 * qi, chains)
    carry = consume(1, 2 * qi + 1, init, [1], diagonal_chain=1)
    older = 2 * qi - 1
    scores_into(1, jnp.maximum(older, 0), chains)
    carry = consume(0, 2 * qi, carry, chains, diagonal_chain=0)
    _, _, carry = lax.while_loop(lambda state: state[1] > 0, pair, (older, needed(older, carry), carry))
    for u in chains:
        acc = carry[u][1]
        out = acc[:HEAD_DIM, :] / acc[HEAD_DIM:HEAD_DIM + 1, :]
        o_ref[u * blk:(u + 1) * blk, :] = out.T.astype(o_ref.dtype)


def _fox_bias_lanes(pieces, bsz, s):
    p = pieces.astype(BF16).transpose(1, 3, 2, 0)
    ones = jnp.ones_like(p)
    pad = jnp.zeros((bsz, s, N_HEADS, FOX_BIAS_LANES - 6), BF16)
    qx = jnp.concatenate([p, ones, pad], axis=-1).reshape(bsz * s, LANES)
    kx = jnp.concatenate([ones, -p, pad], axis=-1).reshape(bsz * s, LANES)
    return qx, kx


def _fox_attention(proj, pieces, qk_bound, bsz, s):
    t = proj.shape[0]
    blk = 512
    tq = FOX_CHAINS * blk
    assert s % tq == 0
    nb, nq = s // blk, s // tq
    fox_w = N_HEADS * HEAD_DIM
    qx, kx = _fox_bias_lanes(pieces, bsz, s)
    f2 = (pieces[0] + pieces[1] + pieces[2]).reshape(bsz * N_HEADS, s)
    f_first_query = f2[:, ::tq]
    f_last_key = f2[:, blk - 1::blk]
    vt = proj[:, 2 * fox_w:3 * fox_w].reshape(bsz, nb, blk, N_HEADS, HEAD_DIM).transpose(0, 3, 1, 4, 2)
    pipelined = 3 * _nbytes((s, HEAD_DIM), BF16) + 3 * _nbytes((tq, HEAD_DIM), BF16)
    resident = (2 + 4) * FOX_CHAINS * _nbytes((blk, blk), F32)
    smem = pl.BlockSpec(memory_space=pltpu.SMEM)
    return pl.pallas_call(
        functools.partial(_fox_kernel, blk=blk),
        grid=(bsz, N_HEADS, nq),
        in_specs=[
            smem, smem, smem,
            pl.BlockSpec((tq, HEAD_DIM), lambda b, h, i: (b * nq + i, h)),
            pl.BlockSpec((tq, LANES), lambda b, h, i: (b * nq + i, 0)),
            pl.BlockSpec((s, HEAD_DIM), lambda b, h, i: (b, N_HEADS + h)),
            pl.BlockSpec((s, LANES), lambda b, h, i: (b, 0)),
            pl.BlockSpec((1, 1, nb, HEAD_DIM, blk), lambda b, h, i: (b, h, 0, 0, 0)),
        ],
        out_specs=pl.BlockSpec((tq, HEAD_DIM), lambda b, h, i: (b * nq + i, h)),
        out_shape=jax.ShapeDtypeStruct((t, fox_w), BF16),
        scratch_shapes=[pltpu.VMEM((2 * HEAD_DIM, tq), BF16),
                        pltpu.VMEM((2, FOX_CHAINS, blk, blk), F32),
                        pltpu.VMEM((2, FOX_CHAINS, 1, blk), F32)],
        compiler_params=pltpu.CompilerParams(
            dimension_semantics=("parallel", "parallel", "arbitrary"),
            vmem_limit_bytes=_vmem_limit(pipelined, resident)),
        name="fox_attention",
    )(qk_bound.reshape(1).astype(F32), f_first_query, f_last_key, proj, qx, proj, kx, vt)


def _retention_tables(chunk):
    h = np.arange(N_HEADS, dtype=np.float64)
    log_g = np.log1p(-np.exp2(-5.0 - h))
    pos = np.arange(chunk, dtype=np.float64)
    rel = pos[:, None] - pos[None, :]
    intra = np.where(rel >= 0, np.exp(log_g[:, None, None] * np.maximum(rel, 0.0)), 0.0)
    q_dec = np.exp(log_g[:, None] * (pos + 1.0))[..., None] * np.ones((1, 1, LANES))
    k_dec = np.exp(log_g[:, None] * (chunk - 1.0 - pos))[..., None] * np.ones((1, 1, LANES))
    c_dec = np.exp(log_g * chunk)[:, None, None] * np.ones((1, 1, LANES))
    return tuple(jnp.asarray(a, F32) for a in (intra, q_dec, k_dec, c_dec))


def _rope(t, cos, sin):
    return t * cos + pltpu.roll(t, HEAD_DIM // 2, 1) * sin


def _retention_kernel(q_ref, k_ref, v_ref, g_ref, cos_ref, sin_ref, intra_ref, qd_ref, kd_ref, cd_ref,
                      gr_ref, o_ref, state_scr, *, chunk):
    @pl.when(pl.program_id(2) == 0)
    def _():
        state_scr[...] = jnp.zeros_like(state_scr)

    intra, q_dec, k_dec, c_dec = intra_ref[0], qd_ref[0], kd_ref[0], cd_ref[0]
    state = state_scr[...]
    for c in range(q_ref.shape[0] // chunk):
        rows = slice(c * chunk, (c + 1) * chunk)
        cos, sin = cos_ref[rows, :], sin_ref[rows, :]
        q = _rope(q_ref[rows, :].astype(F32), cos, sin)
        k = _rope(k_ref[rows, :].astype(F32), cos, sin) * (HEAD_DIM ** -0.5)
        v = v_ref[rows, :]
        sc = lax.dot_general(q.astype(BF16), k.astype(BF16), (((1,), (1,)), ((), ())),
                             preferred_element_type=F32) * intra
        o = jnp.dot(sc.astype(BF16), v, preferred_element_type=F32)
        o += jnp.dot((q * q_dec).astype(BF16), state.astype(BF16), preferred_element_type=F32)
        state = c_dec * state + lax.dot_general((k * k_dec).astype(BF16), v, (((0,), (0,)), ((), ())),
                                                preferred_element_type=F32)
        y = o - jnp.mean(o, axis=-1, keepdims=True)
        y = y * lax.rsqrt(jnp.mean(y * y, axis=-1, keepdims=True) + EPS)
        gate = g_ref[rows, :].astype(F32)
        o_ref[rows, :] = (y * gr_ref[...] * (gate * _sigmoid(gate))).astype(o_ref.dtype)
    state_scr[...] = state


def _retention(proj, col0, g_ret, cos_t, sin_t, bsz, s):
    t = proj.shape[0]
    chunk = min(RET_CHUNK, s)
    lb = min(1024, s)
    nb = s // lb
    intra, q_dec, k_dec, c_dec = _retention_tables(chunk)

    def col(group):
        return pl.BlockSpec((lb, HEAD_DIM), lambda b, h, i: (b * nb + i, col0 + group * N_HEADS + h))

    def table(shape):
        return pl.BlockSpec((1,) + shape, lambda b, h, i: (h, 0, 0))

    pipelined = 5 * _nbytes((lb, HEAD_DIM), BF16) + 2 * _nbytes((lb, HEAD_DIM), F32)
    resident = 2 * (_nbytes((chunk, chunk), F32) + 2 * _nbytes((chunk, LANES), F32)) + 8 * _nbytes((chunk, chunk), F32)
    return pl.pallas_call(
        functools.partial(_retention_kernel, chunk=chunk),
        grid=(bsz, N_HEADS, nb),
        in_specs=[
            col(0), col(1), col(2), col(3),
            pl.BlockSpec((lb, HEAD_DIM), lambda b, h, i: (i, 0)),
            pl.BlockSpec((lb, HEAD_DIM), lambda b, h, i: (i, 0)),
            table((chunk, chunk)), table((chunk, LANES)), table((chunk, LANES)), table((1, LANES)),
            pl.BlockSpec((1, HEAD_DIM), lambda b, h, i: (0, h)),
        ],
        out_specs=pl.BlockSpec((lb, HEAD_DIM), lambda b, h, i: (b * nb + i, h)),
        out_shape=jax.ShapeDtypeStruct((t, N_HEADS * HEAD_DIM), BF16),
        scratch_shapes=[pltpu.VMEM((HEAD_DIM, HEAD_DIM), F32)],
        compiler_params=pltpu.CompilerParams(
            dimension_semantics=("parallel", "parallel", "arbitrary"),
            vmem_limit_bytes=_vmem_limit(pipelined, resident)),
        name="retention",
    )(proj, proj, proj, proj, cos_t, sin_t, intra, q_dec, k_dec, c_dec, g_ret.reshape(1, -1))


def _conv_silu(cur, tail, w):
    row8 = lax.broadcasted_iota(jnp.int32, (8, cur.shape[1]), 0)
    out = cur * w[CONV_W - 1:CONV_W, :]
    for shift in range(1, CONV_W):
        rolled = pltpu.roll(cur, shift, 0)
        head = jnp.where(row8 < shift, pltpu.roll(tail, shift, 0), rolled[:8, :])
        shifted = jnp.concatenate([head, rolled[8:, :]], axis=0)
        out += shifted * w[CONV_W - 1 - shift:CONV_W - shift, :]
    return out * _sigmoid(out)


def _split3(x):
    hi = x.astype(BF16).astype(F32)
    rest = x - hi
    mid = rest.astype(BF16).astype(F32)
    return hi, mid, (rest - mid).astype(BF16).astype(F32)


def _rows16(assign):
    sub = lax.broadcasted_iota(jnp.int32, (BF16_SUBLANES, LANES), 0)
    out = jnp.zeros((BF16_SUBLANES, LANES), F32)
    for r, val in assign.items():
        out = jnp.where(sub == r, val, out)
    return out


def _mlstm_kernel(q_ref, k_ref, v_ref, og_ref, wq_ref, wk_ref, a_ref, b_ref, am_ref, gh_ref, o_ref,
                  c_scr, m_scr, tail_scr, *, chunk):
    @pl.when(pl.program_id(2) == 0)
    def _():
        c_scr[...] = jnp.zeros_like(c_scr)
        m_scr[...] = jnp.zeros_like(m_scr)
        tail_scr[...] = jnp.zeros_like(tail_scr)

    dv = MLSTM_V
    causal = (lax.broadcasted_iota(jnp.int32, (chunk, chunk), 1)
              <= lax.broadcasted_iota(jnp.int32, (chunk, chunk), 0))
    ones_cols = jnp.ones((chunk, LANES), BF16)
    mean_cols = jnp.ones((dv, LANES), BF16)
    rep_g = _rows16({0: 1.0, 1: 1.0, 2: 1.0})
    rep_b = _rows16({3: 1.0, 4: 1.0, 5: 1.0})
    rep_a = _rows16({6: 1.0, 7: 1.0, 8: 1.0})

    chunks = range(q_ref.shape[0] // chunk)
    halves = (slice(0, LANES), slice(LANES, dv))

    def rows(c):
        return slice(c * chunk, (c + 1) * chunk)

    tail = tail_scr[...]
    w_qk = jnp.concatenate([wq_ref[...], wk_ref[...]], axis=1)
    q_b, k_f, sc_raw, v_aug, b_row, a_row, a_cummax, b_last, lw_max = ([] for _ in range(9))
    for c in chunks:
        cur = jnp.concatenate([q_ref[rows(c), :], k_ref[rows(c), :]], axis=1).astype(F32)
        qk = _conv_silu(cur, tail, w_qk)
        tail = cur[chunk - 8:, :]
        q_b.append(qk[:, :MLSTM_QK].astype(BF16))
        k_f.append(qk[:, MLSTM_QK:] * (MLSTM_QK ** -0.5))
        sc_raw.append(lax.dot_general(q_b[c], k_f[c].astype(BF16), (((1,), (1,)), ((), ())),
                                      preferred_element_type=F32))
        v_aug.append(jnp.concatenate([v_ref[rows(c), :], ones_cols], axis=1))
        b_row.append(b_ref[0, :, rows(c)])
        a_row.append(a_ref[0, :, rows(c)])
        a_cummax.append(am_ref[0, :, rows(c)])
        b_last.append(b_row[c][:, chunk - 1:chunk])
        lw_max.append(b_last[c] + a_cummax[c][:, chunk - 1:chunk])
    tail_scr[...] = tail

    m_in, m_out = [], []
    m_st = m_scr[0:1, 0:1]
    for c in chunks:
        m_in.append(m_st)
        m_st = jnp.maximum(b_last[c] + m_st, lw_max[c])
        m_out.append(m_st)
    m_scr[...] = jnp.broadcast_to(m_st, m_scr.shape)

    rep = []
    for c in chunks:
        g3, b3, a3 = _split3(jnp.maximum(a_cummax[c], m_in[c])), _split3(b_row[c]), _split3(a_row[c])
        lhs = _rows16({0: g3[0], 1: g3[1], 2: g3[2], 3: b3[0], 4: b3[1], 5: b3[2],
                       6: a3[0], 7: a3[1], 8: a3[2], 9: 1.0, 10: 1.0, 11: 1.0})
        rep_e = _rows16({0: -1.0, 1: -1.0, 2: -1.0, 9: a3[0], 10: a3[1], 11: a3[2]})
        rhs = jnp.concatenate([rep_e, rep_g, rep_b, rep_a], axis=1)
        rep.append(lax.dot_general(lhs.astype(BF16), rhs.astype(BF16), (((0,), (0,)), ((), ())),
                                   preferred_element_type=F32))

    num, inc = [], []
    for c in chunks:
        e_mat, a_rep = rep[c][:, 0:LANES], rep[c][:, 3 * LANES:]
        sc = jnp.where(causal, sc_raw[c] * jnp.exp(e_mat), 0.0)
        num.append(jnp.dot(sc.astype(BF16), v_aug[c], preferred_element_type=F32))
        kw = k_f[c] * jnp.exp(b_last[c] + a_rep - m_out[c])
        inc.append(lax.dot_general(kw.astype(BF16), v_aug[c], (((0,), (0,)), ((), ())),
                                   preferred_element_type=F32))

    c_in = []
    c_aug = c_scr[...]
    for c in chunks:
        c_in.append(c_aug.astype(BF16))
        c_aug = jnp.exp(b_last[c] + m_in[c] - m_out[c]) * c_aug + inc[c]
    c_scr[...] = c_aug

    hid, sq_sum = [], []
    for c in chunks:
        g_rep, b_rep = rep[c][:, LANES:2 * LANES], rep[c][:, 2 * LANES:3 * LANES]
        w_inter = jnp.exp(m_in[c] - g_rep)
        inter = jnp.dot(q_b[c], c_in[c], preferred_element_type=F32)
        den = num[c][:, dv:] + w_inter * inter[:, dv:]
        inv = 1.0 / jnp.maximum(jnp.abs(den), jnp.exp(-(b_rep + g_rep)))
        hid.append([(num[c][:, sl] + w_inter * inter[:, sl]) * inv for sl in halves])
        sq = jnp.concatenate([hh * hh for hh in hid[c]], axis=1).astype(BF16)
        sq_sum.append(jnp.dot(sq, mean_cols, preferred_element_type=F32))

    for c in chunks:
        scale = lax.rsqrt(sq_sum[c] * (1.0 / dv) + EPS)
        for part, sl in zip(hid[c], halves):
            gate = _sigmoid(og_ref[rows(c), sl].astype(F32))
            o_ref[rows(c), sl] = (part * scale * gh_ref[:, sl] * gate).astype(o_ref.dtype)


def _mlstm(proj, conv_w, gate_rows, g_h, bsz, s):
    t = proj.shape[0]
    chunk = MLSTM_CHUNK
    assert chunk == LANES and MLSTM_QK == LANES
    lb = min(1024, s)
    nb = s // lb
    vcol = 2 * N_HEADS * MLSTM_QK // MLSTM_V
    gates = pl.BlockSpec((1, 1, lb), lambda b, h, i: (b * N_HEADS + h, 0, i))
    pipelined = 2 * _nbytes((lb, MLSTM_QK), BF16) + 3 * _nbytes((lb, MLSTM_V), BF16)
    resident = 2 * _nbytes((MLSTM_QK, MLSTM_V + LANES), F32) + 16 * _nbytes((chunk, MLSTM_V + LANES), F32)
    return pl.pallas_call(
        functools.partial(_mlstm_kernel, chunk=chunk),
        grid=(bsz, N_HEADS, nb),
        in_specs=[
            pl.BlockSpec((lb, MLSTM_QK), lambda b, h, i: (b * nb + i, h)),
            pl.BlockSpec((lb, MLSTM_QK), lambda b, h, i: (b * nb + i, N_HEADS + h)),
            pl.BlockSpec((lb, MLSTM_V), lambda b, h, i: (b * nb + i, vcol + h)),
            pl.BlockSpec((lb, MLSTM_V), lambda b, h, i: (b * nb + i, vcol + N_HEADS + h)),
            pl.BlockSpec((CONV_W, MLSTM_QK), lambda b, h, i: (0, h)),
            pl.BlockSpec((CONV_W, MLSTM_QK), lambda b, h, i: (0, N_HEADS + h)),
            gates, gates, gates,
            pl.BlockSpec((1, MLSTM_V), lambda b, h, i: (0, h)),
        ],
        out_specs=pl.BlockSpec((lb, MLSTM_V), lambda b, h, i: (b * nb + i, h)),
        out_shape=jax.ShapeDtypeStruct((t, N_HEADS * MLSTM_V), BF16),
        scratch_shapes=[
            pltpu.VMEM((MLSTM_QK, MLSTM_V + LANES), F32),
            pltpu.VMEM((8, LANES), F32),
            pltpu.VMEM((8, 2 * MLSTM_QK), F32),
        ],
        compiler_params=pltpu.CompilerParams(
            dimension_semantics=("parallel", "parallel", "arbitrary"),
            vmem_limit_bytes=_vmem_limit(pipelined, resident)),
        name="mlstm",
    )(proj, proj, proj, proj, conv_w, conv_w,
      *(r.reshape(bsz * N_HEADS, 1, s) for r in gate_rows), g_h.reshape(1, -1))


XATTN_ROW_PARTS = 2


def _xattn_out_kernel(q_ref, k_ref, v_ref, w_ref, x_ref, o_ref):
    tm, d = q_ref.shape
    dh = d // N_XATTN
    part = tm // XATTN_ROW_PARTS
    heads = [slice(h * dh, (h + 1) * dh) for h in range(N_XATTN)]
    parts = [slice(r * part, (r + 1) * part) for r in range(XATTN_ROW_PARTS)]
    scores = {(r, h): lax.dot_general(q_ref[parts[r], heads[h]], k_ref[:, heads[h]], (((1,), (1,)), ((), ())),
                                      preferred_element_type=F32)
              for r in range(XATTN_ROW_PARTS) for h in range(N_XATTN)}
    for r in range(XATTN_ROW_PARTS):
        outs = []
        for h in range(N_XATTN):
            s = scores[(r, h)]
            p = jnp.exp(s - jnp.max(s, axis=-1, keepdims=True))
            p = p / jnp.sum(p, axis=-1, keepdims=True)
            outs.append(jnp.dot(p.astype(BF16), v_ref[:, heads[h]], preferred_element_type=F32).astype(BF16))
        attn = jnp.concatenate(outs, axis=1)
        o_ref[parts[r], :] = x_ref[parts[r], :] + jnp.dot(attn, w_ref[...], preferred_element_type=F32)


def _xattn_out(q, kv, w, x, s, n_mem):
    t, d = q.shape
    tm = min(512, s)
    per_seq = s // tm
    pipelined = (_nbytes((tm, d), BF16) + 2 * _nbytes((n_mem, d), BF16) + _nbytes((d, d), BF16)
                 + 2 * _nbytes((tm, d), F32))
    resident = (N_XATTN * XATTN_ROW_PARTS + 4) * _nbytes((tm // XATTN_ROW_PARTS, n_mem), F32) + _nbytes((tm, d), F32)
    return pl.pallas_call(
        _xattn_out_kernel,
        grid=(t // tm,),
        in_specs=[
            pl.BlockSpec((tm, d), lambda i: (i, 0)),
            pl.BlockSpec((n_mem, d), lambda i: (i // per_seq, 0)),
            pl.BlockSpec((n_mem, d), lambda i: (i // per_seq, 1)),
            pl.BlockSpec((d, d), lambda i: (0, 0)),
            pl.BlockSpec((tm, d), lambda i: (i, 0)),
        ],
        out_specs=pl.BlockSpec((tm, d), lambda i: (i, 0)),
        out_shape=jax.ShapeDtypeStruct((t, d), F32),
        compiler_params=pltpu.CompilerParams(
            dimension_semantics=("parallel",),
            vmem_limit_bytes=_vmem_limit(pipelined, resident)),
        name="xattn_out",
    )(q, kv, kv, w, x)


def _gate_weight(w_cols):
    return jnp.pad(w_cols, ((0, 0), (0, LANES - w_cols.shape[1]))).astype(BF16)


def _rope_tables(s):
    inv = 1.0 / (ROPE_BASE ** (jnp.arange(0, HEAD_DIM, 2, dtype=F32) / HEAD_DIM))
    ang = jnp.arange(s, dtype=F32)[:, None] * inv[None, :]
    cos, sin = jnp.cos(ang), jnp.sin(ang)
    return jnp.concatenate([cos, cos], axis=-1), jnp.concatenate([-sin, sin], axis=-1)


def _even_mixer(x, g, w_in, b_f, g_q, g_k, g_ret, w_out, rope, bsz, s):
    fox_w = N_HEADS * HEAD_DIM
    w_main = jnp.concatenate([w_in[:, :3 * fox_w], w_in[:, 3 * fox_w + N_HEADS:]], axis=1).astype(BF16)
    w_gate = w_in[:, 3 * fox_w:3 * fox_w + LANES].astype(BF16)
    qk_gain = jnp.concatenate([jnp.tile(g_q * (HEAD_DIM ** -0.5 * LOG2E), N_HEADS), jnp.tile(g_k, N_HEADS)])
    proj, gate = _norm_matmul(x, g, w_main, w_gate=w_gate, group=HEAD_DIM, group_gain=qk_gain,
                              n_norm_cols=2 * fox_w)
    pieces = _fox_gates(gate[:N_HEADS], b_f, bsz, s)
    qk_bound = 1.01 * HEAD_DIM ** 0.5 * LOG2E * jnp.max(jnp.abs(g_q)) * jnp.max(jnp.abs(g_k))
    ya = _fox_attention(proj, pieces, qk_bound, bsz, s)
    yb = _retention(proj, 3 * N_HEADS, g_ret, rope[0], rope[1], bsz, s)
    return _out_proj(ya, 0, yb, 0, w_out.astype(BF16), x)


def _odd_mixer(x, g, w_in, conv_w, b_i, b_f, g_h, w_out, bsz, s):
    n_main = 2 * N_HEADS * MLSTM_QK + 2 * N_HEADS * MLSTM_V
    proj, gate = _norm_matmul(x, g, w_in[:, :n_main].astype(BF16), w_gate=_gate_weight(w_in[:, n_main:]))
    gate_rows = _mlstm_gates(gate[:N_HEADS], gate[N_HEADS:2 * N_HEADS], b_i, b_f, bsz, s)
    y = _mlstm(proj, conv_w, gate_rows, g_h, bsz, s)
    return _out_proj(y, 0, y, 1, w_out.astype(BF16), x)


def _cross_attention(x, mem, g_x, g_m, wq, wk, wv, wo, g_q, g_k, s):
    d = x.shape[1]
    dh = d // N_XATTN
    n_mem = mem.shape[0] * s // x.shape[0]
    kv = _norm_matmul(mem, g_m, jnp.concatenate([wk, wv], axis=1).astype(BF16),
                      group=dh, group_gain=jnp.tile(g_k, N_XATTN), n_norm_cols=d)
    q = _norm_matmul(x, g_x, wq.astype(BF16), group=dh, group_gain=jnp.tile(g_q * dh ** -0.5, N_XATTN),
                     n_norm_cols=d)
    return _xattn_out(q, kv, wo.astype(BF16), x, s, n_mem)


def kernel(x, mem, norm_mix, norm_xattn, norm_mem, norm_ffn, ev_w_in, ev_b_f, ev_g_q, ev_g_k, ev_g_ret, ev_w_out, od_w_in, od_conv, od_b_i, od_b_f, od_g_h, od_w_out, xa_wq, xa_wk, xa_wv, xa_wo, xa_g_q, xa_g_k, ffn_w1, ffn_w2):
    bsz, s, d = x.shape
    depth = norm_mix.shape[0]
    xf = x.reshape(bsz * s, d)
    memf = mem.reshape(-1, d)
    rope = _rope_tables(s)
    for l in range(depth):
        if l % 2 == 0:
            e = l // 2
            xf = _even_mixer(xf, norm_mix[l], ev_w_in[e], ev_b_f[e], ev_g_q[e], ev_g_k[e], ev_g_ret[e],
                             ev_w_out[e], rope, bsz, s)
        else:
            o = l // 2
            xf = _odd_mixer(xf, norm_mix[l], od_w_in[o], od_conv[o], od_b_i[o], od_b_f[o], od_g_h[o],
                            od_w_out[o], bsz, s)
        xf = _cross_attention(xf, memf, norm_xattn[l], norm_mem[l], xa_wq[l], xa_wk[l], xa_wv[l], xa_wo[l],
                              xa_g_q[l], xa_g_k[l], s)
        xf = _mlp(xf, norm_ffn[l], ffn_w1[l].astype(BF16), ffn_w2[l].astype(BF16))
    return xf.reshape(bsz, s, d)
```

```python
import functools

import numpy as np
import jax
import jax.numpy as jnp
from jax import lax
from jax.experimental import pallas as pl
from jax.experimental.pallas import tpu as pltpu

F32 = jnp.float32
BF16 = jnp.bfloat16

EPS = 1e-6
LOG2E = 1.4426950408889634
LANES = 128
BF16_SUBLANES = 16
HEAD_DIM = 128
N_HEADS = 8
MLSTM_QK = 128
MLSTM_V = 256
N_XATTN = 4
CONV_W = 4
ROPE_BASE = 10000.0
MLSTM_CHUNK = 128
RET_CHUNK = 256
V7X_VMEM_BUDGET = 60 * 1024 * 1024
NORM_ROW_PARTS = 4


def _vmem_limit(pipelined_bytes, resident_bytes):
    return int(min(2 * pipelined_bytes + resident_bytes + (4 << 20), V7X_VMEM_BUDGET))


def _nbytes(shape, dtype):
    return int(np.prod(shape)) * jnp.dtype(dtype).itemsize


def _rms_rows(x, g):
    return x * lax.rsqrt(jnp.mean(x * x, axis=-1, keepdims=True) + EPS) * g


def _sigmoid(x):
    return 0.5 * jnp.tanh(0.5 * x) + 0.5


def _log_sigmoid(x):
    return jnp.minimum(x, 0.0) - jnp.log1p(jnp.exp(-jnp.abs(x)))


def _norm_matmul_kernel(*refs, group, n_norm_tiles, has_gate):
    x_ref, g_ref, w_ref = refs[:3]
    pos = 3
    wg_ref = eg_ref = gate_ref = None
    if has_gate:
        wg_ref = refs[pos]
        pos += 1
    if group:
        eg_ref = refs[pos]
        pos += 1
    o_ref = refs[pos]
    pos += 1
    if has_gate:
        gate_ref = refs[pos]
        pos += 1
    h_scr, gate_scr = refs[pos], refs[pos + 1]

    j = pl.program_id(1)
    tm, tn = o_ref.shape

    def store(acc, rows, normed):
        if normed:
            for c in range(tn // group):
                sl = slice(c * group, (c + 1) * group)
                o_ref[rows, sl] = _rms_rows(acc[:, sl], eg_ref[:, sl]).astype(o_ref.dtype)
        else:
            o_ref[rows, :] = acc.astype(o_ref.dtype)

    @pl.when(j == 0)
    def _():
        part = tm // NORM_ROW_PARTS
        for r in range(NORM_ROW_PARTS):
            rows = slice(r * part, (r + 1) * part)
            h = _rms_rows(x_ref[rows, :], g_ref[...]).astype(BF16)
            h_scr[rows, :] = h
            if has_gate:
                gate_scr[rows, :] = jnp.dot(h, wg_ref[...], preferred_element_type=F32)
                gate_ref[:, rows] = gate_scr[rows, :].T
            store(jnp.dot(h, w_ref[...], preferred_element_type=F32), rows, n_norm_tiles > 0)

    @pl.when(j > 0)
    def _():
        acc = jnp.dot(h_scr[...], w_ref[...], preferred_element_type=F32)
        if not group:
            store(acc, slice(None), False)
            return

        @pl.when(j < n_norm_tiles)
        def _():
            store(acc, slice(None), True)

        @pl.when(j >= n_norm_tiles)
        def _():
            store(acc, slice(None), False)


def _norm_matmul(x, g, w, *, w_gate=None, group=0, group_gain=None, n_norm_cols=0):
    t, d = x.shape
    n = w.shape[1]
    tm = min(1024, t)
    tn = 2048 if (n % 2048 == 0 and n_norm_cols % 2048 == 0) else 1024
    has_gate = w_gate is not None
    n_norm_tiles = n_norm_cols // tn if group else 0
    in_specs = [
        pl.BlockSpec((tm, d), lambda i, j: (i, 0)),
        pl.BlockSpec((1, d), lambda i, j: (0, 0)),
        pl.BlockSpec((d, tn), lambda i, j: (0, j)),
    ]
    args = [x, g.reshape(1, d), w]
    if has_gate:
        in_specs.append(pl.BlockSpec((d, LANES), lambda i, j: (0, 0)))
        args.append(w_gate)
    if group:
        last = n_norm_tiles - 1
        in_specs.append(pl.BlockSpec((1, tn), lambda i, j: (0, jnp.minimum(j, last))))
        args.append(group_gain.reshape(1, n_norm_cols))
    out_shape = [jax.ShapeDtypeStruct((t, n), BF16)]
    out_specs = [pl.BlockSpec((tm, tn), lambda i, j: (i, j))]
    if has_gate:
        out_shape.append(jax.ShapeDtypeStruct((LANES, t), F32))
        out_specs.append(pl.BlockSpec((LANES, tm), lambda i, j: (0, i)))
    pipelined = (_nbytes((tm, d), F32) + _nbytes((d, tn), BF16) + _nbytes((tm, tn), BF16)
                 + _nbytes((tm, LANES), F32) + _nbytes((d, LANES), BF16))
    resident = _nbytes((tm, d), BF16) + 3 * _nbytes((tm, tn), F32) + _nbytes((tm, d), F32)
    outs = pl.pallas_call(
        functools.partial(_norm_matmul_kernel, group=group, n_norm_tiles=n_norm_tiles, has_gate=has_gate),
        grid=(t // tm, n // tn),
        in_specs=in_specs,
        out_specs=out_specs,
        out_shape=out_shape,
        scratch_shapes=[pltpu.VMEM((tm, d), BF16), pltpu.VMEM((tm, LANES), F32)],
        compiler_params=pltpu.CompilerParams(
            dimension_semantics=("parallel", "arbitrary"),
            vmem_limit_bytes=_vmem_limit(pipelined, resident)),
        name="norm_matmul",
    )(*args)
    return outs if has_gate else outs[0]


def _out_proj_kernel(al_ref, ar_ref, w_ref, x_ref, o_ref):
    kl = al_ref.shape[1]
    acc = jnp.dot(al_ref[...], w_ref[:kl, :], preferred_element_type=F32)
    acc += jnp.dot(ar_ref[...], w_ref[kl:, :], preferred_element_type=F32)
    o_ref[...] = x_ref[...] + acc


def _out_proj(a_left, col_left, a_right, col_right, w, x):
    t, d = x.shape
    k = w.shape[0]
    kh = k // 2
    tm = min(512, t)
    pipelined = 2 * _nbytes((tm, kh), BF16) + 2 * _nbytes((tm, d), F32) + _nbytes((k, d), BF16)
    return pl.pallas_call(
        _out_proj_kernel,
        grid=(t // tm,),
        in_specs=[
            pl.BlockSpec((tm, kh), lambda i: (i, col_left)),
            pl.BlockSpec((tm, kh), lambda i: (i, col_right)),
            pl.BlockSpec((k, d), lambda i: (0, 0)),
            pl.BlockSpec((tm, d), lambda i: (i, 0)),
        ],
        out_specs=pl.BlockSpec((tm, d), lambda i: (i, 0)),
        out_shape=jax.ShapeDtypeStruct((t, d), F32),
        compiler_params=pltpu.CompilerParams(
            dimension_semantics=("parallel",),
            vmem_limit_bytes=_vmem_limit(pipelined, _nbytes((tm, d), F32))),
        name="out_proj",
    )(a_left, a_right, w, x)


def _mlp_kernel(x_ref, g_ref, w1_ref, w2_ref, o_ref, h_scr):
    def hidden_out(h):
        u = jnp.maximum(jnp.dot(h, w1_ref[...], preferred_element_type=F32), 0.0)
        return jnp.dot((u * u).astype(BF16), w2_ref[...], preferred_element_type=F32)

    @pl.when(pl.program_id(1) == 0)
    def _():
        part = x_ref.shape[0] // NORM_ROW_PARTS
        for r in range(NORM_ROW_PARTS):
            rows = slice(r * part, (r + 1) * part)
            x = x_ref[rows, :]
            h = _rms_rows(x, g_ref[...]).astype(BF16)
            h_scr[rows, :] = h
            o_ref[rows, :] = x + hidden_out(h)

    @pl.when(pl.program_id(1) > 0)
    def _():
        o_ref[...] += hidden_out(h_scr[...])


def _mlp(x, g, w1, w2):
    t, d = x.shape
    f = w1.shape[1]
    tm = min(1024, t)
    tf = min(1024, f)
    pipelined = 2 * _nbytes((tm, d), F32) + 2 * _nbytes((d, tf), BF16)
    resident = _nbytes((tm, d), BF16) + 2 * _nbytes((tm, tf), F32)
    return pl.pallas_call(
        _mlp_kernel,
        grid=(t // tm, f // tf),
        in_specs=[
            pl.BlockSpec((tm, d), lambda i, j: (i, 0)),
            pl.BlockSpec((1, d), lambda i, j: (0, 0)),
            pl.BlockSpec((d, tf), lambda i, j: (0, j)),
            pl.BlockSpec((tf, d), lambda i, j: (j, 0)),
        ],
        out_specs=pl.BlockSpec((tm, d), lambda i, j: (i, 0)),
        out_shape=jax.ShapeDtypeStruct((t, d), F32),
        scratch_shapes=[pltpu.VMEM((tm, d), BF16)],
        compiler_params=pltpu.CompilerParams(
            dimension_semantics=("parallel", "arbitrary"),
            vmem_limit_bytes=_vmem_limit(pipelined, resident)),
        name="mlp",
    )(x, g.reshape(1, d), w1, w2)


def _upper_tri_ones():
    r = lax.broadcasted_iota(jnp.int32, (LANES, LANES), 0)
    c = lax.broadcasted_iota(jnp.int32, (LANES, LANES), 1)
    return jnp.where(r <= c, 1.0, 0.0).astype(BF16)


def _cumsum_lanes(x, tri):
    hi = x.astype(BF16)
    r1 = x - hi.astype(F32)
    mid = r1.astype(BF16)
    lo = (r1 - mid.astype(F32)).astype(BF16)
    out = jnp.dot(hi, tri, preferred_element_type=F32)
    out += jnp.dot(mid, tri, preferred_element_type=F32)
    out += jnp.dot(lo, tri, preferred_element_type=F32)
    return out


def _fox_gate_kernel(f_ref, b_ref, o_ref, carry_scr):
    @pl.when(pl.program_id(1) == 0)
    def _():
        carry_scr[...] = jnp.zeros_like(carry_scr)

    tri = _upper_tri_ones()
    carry = carry_scr[:, 0:1]
    for c in range(f_ref.shape[1] // LANES):
        sl = slice(c * LANES, (c + 1) * LANES)
        ls = _log_sigmoid(f_ref[:, sl] + b_ref[...])
        cum = _cumsum_lanes(ls, tri) + carry
        carry = cum[:, LANES - 1:LANES]
        c2 = cum * LOG2E
        hi = c2.astype(BF16).astype(F32)
        mid = (c2 - hi).astype(BF16).astype(F32)
        o_ref[0, 0, :, sl] = hi
        o_ref[1, 0, :, sl] = mid
        o_ref[2, 0, :, sl] = ((c2 - hi) - mid).astype(BF16).astype(F32)
    carry_scr[...] = jnp.broadcast_to(carry, carry_scr.shape)


def _fox_gates(f_rows, b_f, bsz, s):
    h = f_rows.shape[0]
    lb = min(2048, s)
    nblk = s // lb
    return pl.pallas_call(
        _fox_gate_kernel,
        grid=(bsz, nblk),
        in_specs=[pl.BlockSpec((h, lb), lambda b, i: (0, b * nblk + i)),
                  pl.BlockSpec((h, 1), lambda b, i: (0, 0))],
        out_specs=pl.BlockSpec((3, 1, h, lb), lambda b, i: (0, b, 0, i)),
        out_shape=jax.ShapeDtypeStruct((3, bsz, h, s), F32),
        scratch_shapes=[pltpu.VMEM((h, LANES), F32)],
        compiler_params=pltpu.CompilerParams(dimension_semantics=("parallel", "arbitrary")),
        name="fox_gates",
    )(f_rows, b_f.reshape(h, 1))


def _prefix_max_lanes(x):
    lane = lax.broadcasted_iota(jnp.int32, x.shape, 1)
    shift = 1
    while shift < LANES:
        x = jnp.maximum(x, jnp.where(lane >= shift, pltpu.roll(x, shift, 1), -jnp.inf))
        shift *= 2
    return x


def _mlstm_gate_kernel(i_ref, f_ref, bi_ref, bf_ref, ao_ref, bo_ref, mo_ref):
    tri = _upper_tri_ones()
    for c in range(f_ref.shape[1] // LANES):
        sl = slice(c * LANES, (c + 1) * LANES)
        b = _cumsum_lanes(_log_sigmoid(f_ref[:, sl] + bf_ref[...]), tri)
        a = i_ref[:, sl] + bi_ref[...] - b
        bo_ref[0, :, sl] = b
        ao_ref[0, :, sl] = a
        mo_ref[0, :, sl] = _prefix_max_lanes(a)


def _mlstm_gates(i_rows, f_rows, b_i, b_f, bsz, s):
    assert MLSTM_CHUNK == LANES
    h = f_rows.shape[0]
    lb = min(2048, s)
    nblk = s // lb
    row_in = pl.BlockSpec((h, lb), lambda b, i: (0, b * nblk + i))
    row = pl.BlockSpec((1, h, lb), lambda b, i: (b, 0, i))
    bias = pl.BlockSpec((h, 1), lambda b, i: (0, 0))
    return pl.pallas_call(
        _mlstm_gate_kernel,
        grid=(bsz, nblk),
        in_specs=[row_in, row_in, bias, bias],
        out_specs=[row, row, row],
        out_shape=[jax.ShapeDtypeStruct((bsz, h, s), F32)] * 3,
        compiler_params=pltpu.CompilerParams(dimension_semantics=("parallel", "parallel")),
        name="mlstm_gates",
    )(i_rows, f_rows, b_i.reshape(h, 1), b_f.reshape(h, 1))


FOX_CHAINS = 2
FOX_BIAS_LANES = LANES // N_HEADS
FOX_SKIP_BITS = 150.0


def _fox_kernel(qkb_ref, fq_ref, fk_ref, q_ref, qx_ref, k_ref, kx_ref, vt_ref, o_ref, qt_scr, st_scr, mx_scr,
                *, blk):
    qi = pl.program_id(2)
    bh = pl.program_id(0) * N_HEADS + pl.program_id(1)
    lane = lax.broadcasted_iota(jnp.int32, qx_ref.shape, 1)
    owner = lax.shift_right_logical(lane, FOX_BIAS_LANES.bit_length() - 1)
    qx = jnp.where(owner == pl.program_id(1), qx_ref[...].astype(F32), 0.0)
    q_aug = jnp.concatenate([q_ref[...].astype(F32), qx], axis=1)
    qt_scr[...] = q_aug.T.astype(BF16)
    chains = range(FOX_CHAINS)
    ones_rows = jnp.ones((BF16_SUBLANES, blk), BF16)

    def col_max(st):
        part = jnp.max(st.reshape(blk // 16, 16, blk), axis=0)
        return jnp.max(part, axis=0, keepdims=True)

    def scores_into(slot, j, which):
        off = pl.multiple_of(j * blk, blk)
        k_aug = jnp.concatenate([k_ref[pl.ds(off, blk), :], kx_ref[pl.ds(off, blk), :]], axis=1)
        for u in which:
            st = jnp.dot(k_aug, qt_scr[:, u * blk:(u + 1) * blk], preferred_element_type=F32)
            st_scr[slot, u] = st
            mx_scr[slot, u] = col_max(st)

    def consume(slot, j, carry, which, diagonal_chain=None):
        vt = jnp.concatenate([vt_ref[0, 0, j], ones_rows], axis=0)
        carry = list(carry)
        for u in which:
            m, acc = carry[u]
            st = st_scr[slot, u]
            if u == diagonal_chain:
                key = lax.broadcasted_iota(jnp.int32, st.shape, 0)
                qry = lax.broadcasted_iota(jnp.int32, st.shape, 1)
                st = jnp.where(key <= qry, st, -1e30)
                mx = col_max(st)
            else:
                mx = mx_scr[slot, u]
            m_new = jnp.maximum(m, mx)
            p = jnp.exp2(st - m_new).astype(BF16)
            acc = jnp.exp2(m - m_new) * acc + jnp.dot(vt, p, preferred_element_type=F32)
            carry[u] = (m_new, acc)
        return tuple(carry)

    f_first_query = fq_ref[bh, qi]
    qkb = qkb_ref[0]

    def needed(a, carry):
        lowest_max = jnp.min(jnp.minimum(carry[0][0], carry[1][0]))
        bound = qkb + f_first_query - fk_ref[bh, jnp.maximum(a, 0)]
        return jnp.logical_and(a >= 1, bound > lowest_max - FOX_SKIP_BITS).astype(jnp.int32)

    def pair(state):
        a, _, carry = state
        scores_into(0, a - 1, chains)
        carry = consume(1, a, carry, chains)
        scores_into(1, jnp.maximum(a - 2, 0), chains)
        carry = consume(0, a - 1, carry, chains)
        return a - 2, needed(a - 2, carry), carry

    init = tuple((jnp.full((1, blk), -1e30, F32), jnp.zeros((HEAD_DIM + BF16_SUBLANES, blk), F32))
                 for _ in chains)
    scores_into(1, 2 * qi + 1, [1])
    scores_into(0, 2 * qi, chains)
    carry = consume(1, 2 * qi + 1, init, [1], diagonal_chain=1)
    older = 2 * qi - 1
    scores_into(1, jnp.maximum(older, 0), chains)
    carry = consume(0, 2 * qi, carry, chains, diagonal_chain=0)
    _, _, carry = lax.while_loop(lambda state: state[1] > 0, pair, (older, needed(older, carry), carry))
    for u in chains:
        acc = carry[u][1]
        out = acc[:HEAD_DIM, :] / acc[HEAD_DIM:HEAD_DIM + 1, :]
        o_ref[u * blk:(u + 1) * blk, :] = out.T.astype(o_ref.dtype)


def _fox_bias_lanes(pieces, bsz, s):
    p = pieces.astype(BF16).transpose(1, 3, 2, 0)
    ones = jnp.ones_like(p)
    pad = jnp.zeros((bsz, s, N_HEADS, FOX_BIAS_LANES - 6), BF16)
    qx = jnp.concatenate([p, ones, pad], axis=-1).reshape(bsz * s, LANES)
    kx = jnp.concatenate([ones, -p, pad], axis=-1).reshape(bsz * s, LANES)
    return qx, kx


def _fox_attention(proj, pieces, qk_bound, bsz, s):
    t = proj.shape[0]
    blk = 512
    tq = FOX_CHAINS * blk
    assert s % tq == 0
    nb, nq = s // blk, s // tq
    fox_w = N_HEADS * HEAD_DIM
    qx, kx = _fox_bias_lanes(pieces, bsz, s)
    f2 = (pieces[0] + pieces[1] + pieces[2]).reshape(bsz * N_HEADS, s)
    f_first_query = f2[:, ::tq]
    f_last_key = f2[:, blk - 1::blk]
    vt = proj[:, 2 * fox_w:3 * fox_w].reshape(bsz, nb, blk, N_HEADS, HEAD_DIM).transpose(0, 3, 1, 4, 2)
    pipelined = 3 * _nbytes((s, HEAD_DIM), BF16) + 3 * _nbytes((tq, HEAD_DIM), BF16)
    resident = (2 + 4) * FOX_CHAINS * _nbytes((blk, blk), F32)
    smem = pl.BlockSpec(memory_space=pltpu.SMEM)
    return pl.pallas_call(
        functools.partial(_fox_kernel, blk=blk),
        grid=(bsz, N_HEADS, nq),
        in_specs=[
            smem, smem, smem,
            pl.BlockSpec((tq, HEAD_DIM), lambda b, h, i: (b * nq + i, h)),
            pl.BlockSpec((tq, LANES), lambda b, h, i: (b * nq + i, 0)),
            pl.BlockSpec((s, HEAD_DIM), lambda b, h, i: (b, N_HEADS + h)),
            pl.BlockSpec((s, LANES), lambda b, h, i: (b, 0)),
            pl.BlockSpec((1, 1, nb, HEAD_DIM, blk), lambda b, h, i: (b, h, 0, 0, 0)),
        ],
        out_specs=pl.BlockSpec((tq, HEAD_DIM), lambda b, h, i: (b * nq + i, h)),
        out_shape=jax.ShapeDtypeStruct((t, fox_w), BF16),
        scratch_shapes=[pltpu.VMEM((2 * HEAD_DIM, tq), BF16),
                        pltpu.VMEM((2, FOX_CHAINS, blk, blk), F32),
                        pltpu.VMEM((2, FOX_CHAINS, 1, blk), F32)],
        compiler_params=pltpu.CompilerParams(
            dimension_semantics=("parallel", "parallel", "arbitrary"),
            vmem_limit_bytes=_vmem_limit(pipelined, resident)),
        name="fox_attention",
    )(qk_bound.reshape(1).astype(F32), f_first_query, f_last_key, proj, qx, proj, kx, vt)


def _retention_tables(chunk):
    h = np.arange(N_HEADS, dtype=np.float64)
    log_g = np.log1p(-np.exp2(-5.0 - h))
    pos = np.arange(chunk, dtype=np.float64)
    rel = pos[:, None] - pos[None, :]
    intra = np.where(rel >= 0, np.exp(log_g[:, None, None] * np.maximum(rel, 0.0)), 0.0)
    q_dec = np.exp(log_g[:, None] * (pos + 1.0))[..., None] * np.ones((1, 1, LANES))
    k_dec = np.exp(log_g[:, None] * (chunk - 1.0 - pos))[..., None] * np.ones((1, 1, LANES))
    c_dec = np.exp(log_g * chunk)[:, None, None] * np.ones((1, 1, LANES))
    return tuple(jnp.asarray(a, F32) for a in (intra, q_dec, k_dec, c_dec))


def _rope(t, cos, sin):
    return t * cos + pltpu.roll(t, HEAD_DIM // 2, 1) * sin


def _retention_kernel(q_ref, k_ref, v_ref, g_ref, cos_ref, sin_ref, intra_ref, qd_ref, kd_ref, cd_ref,
                      gr_ref, o_ref, state_scr, *, chunk):
    @pl.when(pl.program_id(2) == 0)
    def _():
        state_scr[...] = jnp.zeros_like(state_scr)

    intra, q_dec, k_dec, c_dec = intra_ref[0], qd_ref[0], kd_ref[0], cd_ref[0]
    state = state_scr[...]
    for c in range(q_ref.shape[0] // chunk):
        rows = slice(c * chunk, (c + 1) * chunk)
        cos, sin = cos_ref[rows, :], sin_ref[rows, :]
        q = _rope(q_ref[rows, :].astype(F32), cos, sin)
        k = _rope(k_ref[rows, :].astype(F32), cos, sin) * (HEAD_DIM ** -0.5)
        v = v_ref[rows, :]
        sc = lax.dot_general(q.astype(BF16), k.astype(BF16), (((1,), (1,)), ((), ())),
                             preferred_element_type=F32) * intra
        o = jnp.dot(sc.astype(BF16), v, preferred_element_type=F32)
        o += jnp.dot((q * q_dec).astype(BF16), state.astype(BF16), preferred_element_type=F32)
        state = c_dec * state + lax.dot_general((k * k_dec).astype(BF16), v, (((0,), (0,)), ((), ())),
                                                preferred_element_type=F32)
        y = o - jnp.mean(o, axis=-1, keepdims=True)
        y = y * lax.rsqrt(jnp.mean(y * y, axis=-1, keepdims=True) + EPS)
        gate = g_ref[rows, :].astype(F32)
        o_ref[rows, :] = (y * gr_ref[...] * (gate * _sigmoid(gate))).astype(o_ref.dtype)
    state_scr[...] = state


def _retention(proj, col0, g_ret, cos_t, sin_t, bsz, s):
    t = proj.shape[0]
    chunk = min(RET_CHUNK, s)
    lb = min(1024, s)
    nb = s // lb
    intra, q_dec, k_dec, c_dec = _retention_tables(chunk)

    def col(group):
        return pl.BlockSpec((lb, HEAD_DIM), lambda b, h, i: (b * nb + i, col0 + group * N_HEADS + h))

    def table(shape):
        return pl.BlockSpec((1,) + shape, lambda b, h, i: (h, 0, 0))

    pipelined = 5 * _nbytes((lb, HEAD_DIM), BF16) + 2 * _nbytes((lb, HEAD_DIM), F32)
    resident = 2 * (_nbytes((chunk, chunk), F32) + 2 * _nbytes((chunk, LANES), F32)) + 8 * _nbytes((chunk, chunk), F32)
    return pl.pallas_call(
        functools.partial(_retention_kernel, chunk=chunk),
        grid=(bsz, N_HEADS, nb),
        in_specs=[
            col(0), col(1), col(2), col(3),
            pl.BlockSpec((lb, HEAD_DIM), lambda b, h, i: (i, 0)),
            pl.BlockSpec((lb, HEAD_DIM), lambda b, h, i: (i, 0)),
            table((chunk, chunk)), table((chunk, LANES)), table((chunk, LANES)), table((1, LANES)),
            pl.BlockSpec((1, HEAD_DIM), lambda b, h, i: (0, h)),
        ],
        out_specs=pl.BlockSpec((lb, HEAD_DIM), lambda b, h, i: (b * nb + i, h)),
        out_shape=jax.ShapeDtypeStruct((t, N_HEADS * HEAD_DIM), BF16),
        scratch_shapes=[pltpu.VMEM((HEAD_DIM, HEAD_DIM), F32)],
        compiler_params=pltpu.CompilerParams(
            dimension_semantics=("parallel", "parallel", "arbitrary"),
            vmem_limit_bytes=_vmem_limit(pipelined, resident)),
        name="retention",
    )(proj, proj, proj, proj, cos_t, sin_t, intra, q_dec, k_dec, c_dec, g_ret.reshape(1, -1))


def _conv_silu(cur, tail, w):
    row8 = lax.broadcasted_iota(jnp.int32, (8, cur.shape[1]), 0)
    out = cur * w[CONV_W - 1:CONV_W, :]
    for shift in range(1, CONV_W):
        rolled = pltpu.roll(cur, shift, 0)
        head = jnp.where(row8 < shift, pltpu.roll(tail, shift, 0), rolled[:8, :])
        shifted = jnp.concatenate([head, rolled[8:, :]], axis=0)
        out += shifted * w[CONV_W - 1 - shift:CONV_W - shift, :]
    return out * _sigmoid(out)


def _split3(x):
    hi = x.astype(BF16).astype(F32)
    rest = x - hi
    mid = rest.astype(BF16).astype(F32)
    return hi, mid, (rest - mid).astype(BF16).astype(F32)


def _rows16(assign):
    sub = lax.broadcasted_iota(jnp.int32, (BF16_SUBLANES, LANES), 0)
    out = jnp.zeros((BF16_SUBLANES, LANES), F32)
    for r, val in assign.items():
        out = jnp.where(sub == r, val, out)
    return out


def _mlstm_kernel(q_ref, k_ref, v_ref, og_ref, wq_ref, wk_ref, a_ref, b_ref, am_ref, gh_ref, o_ref,
                  c_scr, m_scr, tail_scr, *, chunk):
    @pl.when(pl.program_id(2) == 0)
    def _():
        c_scr[...] = jnp.zeros_like(c_scr)
        m_scr[...] = jnp.zeros_like(m_scr)
        tail_scr[...] = jnp.zeros_like(tail_scr)

    dv = MLSTM_V
    causal = (lax.broadcasted_iota(jnp.int32, (chunk, chunk), 1)
              <= lax.broadcasted_iota(jnp.int32, (chunk, chunk), 0))
    ones_cols = jnp.ones((chunk, LANES), BF16)
    mean_cols = jnp.ones((dv, LANES), BF16)
    rep_g = _rows16({0: 1.0, 1: 1.0, 2: 1.0})
    rep_b = _rows16({3: 1.0, 4: 1.0, 5: 1.0})
    rep_a = _rows16({6: 1.0, 7: 1.0, 8: 1.0})

    chunks = range(q_ref.shape[0] // chunk)
    halves = (slice(0, LANES), slice(LANES, dv))

    def rows(c):
        return slice(c * chunk, (c + 1) * chunk)

    tail = tail_scr[...]
    w_qk = jnp.concatenate([wq_ref[...], wk_ref[...]], axis=1)
    q_b, k_f, sc_raw, v_aug, b_row, a_row, a_cummax, b_last, lw_max = ([] for _ in range(9))
    for c in chunks:
        cur = jnp.concatenate([q_ref[rows(c), :], k_ref[rows(c), :]], axis=1).astype(F32)
        qk = _conv_silu(cur, tail, w_qk)
        tail = cur[chunk - 8:, :]
        q_b.append(qk[:, :MLSTM_QK].astype(BF16))
        k_f.append(qk[:, MLSTM_QK:] * (MLSTM_QK ** -0.5))
        sc_raw.append(lax.dot_general(q_b[c], k_f[c].astype(BF16), (((1,), (1,)), ((), ())),
                                      preferred_element_type=F32))
        v_aug.append(jnp.concatenate([v_ref[rows(c), :], ones_cols], axis=1))
        b_row.append(b_ref[0, :, rows(c)])
        a_row.append(a_ref[0, :, rows(c)])
        a_cummax.append(am_ref[0, :, rows(c)])
        b_last.append(b_row[c][:, chunk - 1:chunk])
        lw_max.append(b_last[c] + a_cummax[c][:, chunk - 1:chunk])
    tail_scr[...] = tail

    m_in, m_out = [], []
    m_st = m_scr[0:1, 0:1]
    for c in chunks:
        m_in.append(m_st)
        m_st = jnp.maximum(b_last[c] + m_st, lw_max[c])
        m_out.append(m_st)
    m_scr[...] = jnp.broadcast_to(m_st, m_scr.shape)

    rep = []
    for c in chunks:
        g3, b3, a3 = _split3(jnp.maximum(a_cummax[c], m_in[c])), _split3(b_row[c]), _split3(a_row[c])
        lhs = _rows16({0: g3[0], 1: g3[1], 2: g3[2], 3: b3[0], 4: b3[1], 5: b3[2],
                       6: a3[0], 7: a3[1], 8: a3[2], 9: 1.0, 10: 1.0, 11: 1.0})
        rep_e = _rows16({0: -1.0, 1: -1.0, 2: -1.0, 9: a3[0], 10: a3[1], 11: a3[2]})
        rhs = jnp.concatenate([rep_e, rep_g, rep_b, rep_a], axis=1)
        rep.append(lax.dot_general(lhs.astype(BF16), rhs.astype(BF16), (((0,), (0,)), ((), ())),
                                   preferred_element_type=F32))

    num, inc = [], []
    for c in chunks:
        e_mat, a_rep = rep[c][:, 0:LANES], rep[c][:, 3 * LANES:]
        sc = jnp.where(causal, sc_raw[c] * jnp.exp(e_mat), 0.0)
        num.append(jnp.dot(sc.astype(BF16), v_aug[c], preferred_element_type=F32))
        kw = k_f[c] * jnp.exp(b_last[c] + a_rep - m_out[c])
        inc.append(lax.dot_general(kw.astype(BF16), v_aug[c], (((0,), (0,)), ((), ())),
                                   preferred_element_type=F32))

    c_in = []
    c_aug = c_scr[...]
    for c in chunks:
        c_in.append(c_aug.astype(BF16))
        c_aug = jnp.exp(b_last[c] + m_in[c] - m_out[c]) * c_aug + inc[c]
    c_scr[...] = c_aug

    hid, sq_sum = [], []
    for c in chunks:
        g_rep, b_rep = rep[c][:, LANES:2 * LANES], rep[c][:, 2 * LANES:3 * LANES]
        w_inter = jnp.exp(m_in[c] - g_rep)
        inter = jnp.dot(q_b[c], c_in[c], preferred_element_type=F32)
        den = num[c][:, dv:] + w_inter * inter[:, dv:]
        inv = 1.0 / jnp.maximum(jnp.abs(den), jnp.exp(-(b_rep + g_rep)))
        hid.append([(num[c][:, sl] + w_inter * inter[:, sl]) * inv for sl in halves])
        sq = jnp.concatenate([hh * hh for hh in hid[c]], axis=1).astype(BF16)
        sq_sum.append(jnp.dot(sq, mean_cols, preferred_element_type=F32))

    for c in chunks:
        scale = lax.rsqrt(sq_sum[c] * (1.0 / dv) + EPS)
        for part, sl in zip(hid[c], halves):
            gate = _sigmoid(og_ref[rows(c), sl].astype(F32))
            o_ref[rows(c), sl] = (part * scale * gh_ref[:, sl] * gate).astype(o_ref.dtype)


def _mlstm(proj, conv_w, gate_rows, g_h, bsz, s):
    t = proj.shape[0]
    chunk = MLSTM_CHUNK
    assert chunk == LANES and MLSTM_QK == LANES
    lb = min(1024, s)
    nb = s // lb
    vcol = 2 * N_HEADS * MLSTM_QK // MLSTM_V
    gates = pl.BlockSpec((1, 1, lb), lambda b, h, i: (b * N_HEADS + h, 0, i))
    pipelined = 2 * _nbytes((lb, MLSTM_QK), BF16) + 3 * _nbytes((lb, MLSTM_V), BF16)
    resident = 2 * _nbytes((MLSTM_QK, MLSTM_V + LANES), F32) + 16 * _nbytes((chunk, MLSTM_V + LANES), F32)
    return pl.pallas_call(
        functools.partial(_mlstm_kernel, chunk=chunk),
        grid=(bsz, N_HEADS, nb),
        in_specs=[
            pl.BlockSpec((lb, MLSTM_QK), lambda b, h, i: (b * nb + i, h)),
            pl.BlockSpec((lb, MLSTM_QK), lambda b, h, i: (b * nb + i, N_HEADS + h)),
            pl.BlockSpec((lb, MLSTM_V), lambda b, h, i: (b * nb + i, vcol + h)),
            pl.BlockSpec((lb, MLSTM_V), lambda b, h, i: (b * nb + i, vcol + N_HEADS + h)),
            pl.BlockSpec((CONV_W, MLSTM_QK), lambda b, h, i: (0, h)),
            pl.BlockSpec((CONV_W, MLSTM_QK), lambda b, h, i: (0, N_HEADS + h)),
            gates, gates, gates,
            pl.BlockSpec((1, MLSTM_V), lambda b, h, i: (0, h)),
        ],
        out_specs=pl.BlockSpec((lb, MLSTM_V), lambda b, h, i: (b * nb + i, h)),
        out_shape=jax.ShapeDtypeStruct((t, N_HEADS * MLSTM_V), BF16),
        scratch_shapes=[
            pltpu.VMEM((MLSTM_QK, MLSTM_V + LANES), F32),
            pltpu.VMEM((8, LANES), F32),
            pltpu.VMEM((8, 2 * MLSTM_QK), F32),
        ],
        compiler_params=pltpu.CompilerParams(
            dimension_semantics=("parallel", "parallel", "arbitrary"),
            vmem_limit_bytes=_vmem_limit(pipelined, resident)),
        name="mlstm",
    )(proj, proj, proj, proj, conv_w, conv_w,
      *(r.reshape(bsz * N_HEADS, 1, s) for r in gate_rows), g_h.reshape(1, -1))


XATTN_ROW_PARTS = 2


def _xattn_out_kernel(q_ref, k_ref, v_ref, w_ref, x_ref, o_ref):
    tm, d = q_ref.shape
    dh = d // N_XATTN
    part = tm // XATTN_ROW_PARTS
    heads = [slice(h * dh, (h + 1) * dh) for h in range(N_XATTN)]
    parts = [slice(r * part, (r + 1) * part) for r in range(XATTN_ROW_PARTS)]
    scores = {(r, h): lax.dot_general(q_ref[parts[r], heads[h]], k_ref[:, heads[h]], (((1,), (1,)), ((), ())),
                                      preferred_element_type=F32)
              for r in range(XATTN_ROW_PARTS) for h in range(N_XATTN)}
    for r in range(XATTN_ROW_PARTS):
        outs = []
        for h in range(N_XATTN):
            s = scores[(r, h)]
            p = jnp.exp(s - jnp.max(s, axis=-1, keepdims=True))
            p = p / jnp.sum(p, axis=-1, keepdims=True)
            outs.append(jnp.dot(p.astype(BF16), v_ref[:, heads[h]], preferred_element_type=F32).astype(BF16))
        attn = jnp.concatenate(outs, axis=1)
        o_ref[parts[r], :] = x_ref[parts[r], :] + jnp.dot(attn, w_ref[...], preferred_element_type=F32)


def _xattn_out(q, kv, w, x, s, n_mem):
    t, d = q.shape
    tm = min(512, s)
    per_seq = s // tm
    pipelined = (_nbytes((tm, d), BF16) + 2 * _nbytes((n_mem, d), BF16) + _nbytes((d, d), BF16)
                 + 2 * _nbytes((tm, d), F32))
    resident = (N_XATTN * XATTN_ROW_PARTS + 4) * _nbytes((tm // XATTN_ROW_PARTS, n_mem), F32) + _nbytes((tm, d), F32)
    return pl.pallas_call(
        _xattn_out_kernel,
        grid=(t // tm,),
        in_specs=[
            pl.BlockSpec((tm, d), lambda i: (i, 0)),
            pl.BlockSpec((n_mem, d), lambda i: (i // per_seq, 0)),
            pl.BlockSpec((n_mem, d), lambda i: (i // per_seq, 1)),
            pl.BlockSpec((d, d), lambda i: (0, 0)),
            pl.BlockSpec((tm, d), lambda i: (i, 0)),
        ],
        out_specs=pl.BlockSpec((tm, d), lambda i: (i, 0)),
        out_shape=jax.ShapeDtypeStruct((t, d), F32),
        compiler_params=pltpu.CompilerParams(
            dimension_semantics=("parallel",),
            vmem_limit_bytes=_vmem_limit(pipelined, resident)),
        name="xattn_out",
    )(q, kv, kv, w, x)


def _gate_weight(w_cols):
    return jnp.pad(w_cols, ((0, 0), (0, LANES - w_cols.shape[1]))).astype(BF16)


def _rope_tables(s):
    inv = 1.0 / (ROPE_BASE ** (jnp.arange(0, HEAD_DIM, 2, dtype=F32) / HEAD_DIM))
    ang = jnp.arange(s, dtype=F32)[:, None] * inv[None, :]
    cos, sin = jnp.cos(ang), jnp.sin(ang)
    return jnp.concatenate([cos, cos], axis=-1), jnp.concatenate([-sin, sin], axis=-1)


def _even_mixer(x, g, w_in, b_f, g_q, g_k, g_ret, w_out, rope, bsz, s):
    fox_w = N_HEADS * HEAD_DIM
    w_main = jnp.concatenate([w_in[:, :3 * fox_w], w_in[:, 3 * fox_w + N_HEADS:]], axis=1).astype(BF16)
    w_gate = w_in[:, 3 * fox_w:3 * fox_w + LANES].astype(BF16)
    qk_gain = jnp.concatenate([jnp.tile(g_q * (HEAD_DIM ** -0.5 * LOG2E), N_HEADS), jnp.tile(g_k, N_HEADS)])
    proj, gate = _norm_matmul(x, g, w_main, w_gate=w_gate, group=HEAD_DIM, group_gain=qk_gain,
                              n_norm_cols=2 * fox_w)
    pieces = _fox_gates(gate[:N_HEADS], b_f, bsz, s)
    qk_bound = 1.01 * HEAD_DIM ** 0.5 * LOG2E * jnp.max(jnp.abs(g_q)) * jnp.max(jnp.abs(g_k))
    ya = _fox_attention(proj, pieces, qk_bound, bsz, s)
    yb = _retention(proj, 3 * N_HEADS, g_ret, rope[0], rope[1], bsz, s)
    return _out_proj(ya, 0, yb, 0, w_out.astype(BF16), x)


def _odd_mixer(x, g, w_in, conv_w, b_i, b_f, g_h, w_out, bsz, s):
    n_main = 2 * N_HEADS * MLSTM_QK + 2 * N_HEADS * MLSTM_V
    proj, gate = _norm_matmul(x, g, w_in[:, :n_main].astype(BF16), w_gate=_gate_weight(w_in[:, n_main:]))
    gate_rows = _mlstm_gates(gate[:N_HEADS], gate[N_HEADS:2 * N_HEADS], b_i, b_f, bsz, s)
    y = _mlstm(proj, conv_w, gate_rows, g_h, bsz, s)
    return _out_proj(y, 0, y, 1, w_out.astype(BF16), x)


def _cross_attention(x, mem, g_x, g_m, wq, wk, wv, wo, g_q, g_k, s):
    d = x.shape[1]
    dh = d // N_XATTN
    n_mem = mem.shape[0] * s // x.shape[0]
    kv = _norm_matmul(mem, g_m, jnp.concatenate([wk, wv], axis=1).astype(BF16),
                      group=dh, group_gain=jnp.tile(g_k, N_XATTN), n_norm_cols=d)
    q = _norm_matmul(x, g_x, wq.astype(BF16), group=dh, group_gain=jnp.tile(g_q * dh ** -0.5, N_XATTN),
                     n_norm_cols=d)
    return _xattn_out(q, kv, wo.astype(BF16), x, s, n_mem)


def kernel(x, mem, norm_mix, norm_xattn, norm_mem, norm_ffn, ev_w_in, ev_b_f, ev_g_q, ev_g_k, ev_g_ret, ev_w_out, od_w_in, od_conv, od_b_i, od_b_f, od_g_h, od_w_out, xa_wq, xa_wk, xa_wv, xa_wo, xa_g_q, xa_g_k, ffn_w1, ffn_w2):
    bsz, s, d = x.shape
    depth = norm_mix.shape[0]
    xf = x.reshape(bsz * s, d)
    memf = mem.reshape(-1, d)
    rope = _rope_tables(s)
    for l in range(depth):
        if l % 2 == 0:
            e = l // 2
            xf = _even_mixer(xf, norm_mix[l], ev_w_in[e], ev_b_f[e], ev_g_q[e], ev_g_k[e], ev_g_ret[e],
                             ev_w_out[e], rope, bsz, s)
        else:
            o = l // 2
            xf = _odd_mixer(xf, norm_mix[l], od_w_in[o], od_conv[o], od_b_i[o], od_b_f[o], od_g_h[o],
                            od_w_out[o], bsz, s)
        xf = _cross_attention(xf, memf, norm_xattn[l], norm_mem[l], xa_wq[l], xa_wk[l], xa_wv[l], xa_wo[l],
                              xa_g_q[l], xa_g_k[l], s)
        xf = _mlp(xf, norm_ffn[l], ffn_w1[l].astype(BF16), ffn_w2[l].astype(BF16))
    return xf.reshape(bsz, s, d)
```
